```python
import jax
import jax.numpy as jnp
from jax import lax
import numpy as np

D_MODEL = 2048
BATCH = 2
SEQ = 4096
DEPTH = 4
DEC_BATCH = 8
DEC_SEQ = 1
PAST_LEN = 16384
PAGE_SIZE = 128

MIXER_ORDER = ('fox', 'gdn', 'ret')
N_MIXERS = len(MIXER_ORDER)
D_FF = 5632
NORM_EPS = 1e-6
GN_EPS = 1e-5
FFN_HALF = 0.5

FOX_HEADS = 16
FOX_HEAD_DIM = D_MODEL // FOX_HEADS
FOX_Q_BLOCK = 128
FOX_BIAS_LO = 2.0
FOX_BIAS_HI = 12.0

GDN_QK_HEADS = 16
GDN_V_HEADS = 32
GDN_DK = 128
GDN_DV = 128
GDN_QK_DIM = GDN_QK_HEADS * GDN_DK
GDN_V_DIM = GDN_V_HEADS * GDN_DV
GDN_CONV_DIM = 2 * GDN_QK_DIM + GDN_V_DIM
GDN_CONV_WIDTH = 4
GDN_CHUNK = 64

RET_HEADS = 8
RET_DK = D_MODEL // RET_HEADS
RET_DV = 2 * RET_DK
RET_V_DIM = RET_HEADS * RET_DV
RET_CHUNK = 128
ROPE_BASE = 10000.0

kernel_name = 'hybrid_fox_gdn_retention_macaron_step'


def _rms_norm(x, g):
    xf = x.astype(jnp.float32)
    y = xf * lax.rsqrt(jnp.mean(xf * xf, axis=-1, keepdims=True) + NORM_EPS)
    return (y * g.astype(jnp.float32)).astype(x.dtype)


def _l2_norm(x):
    xf = x.astype(jnp.float32)
    return xf * lax.rsqrt(jnp.sum(xf * xf, axis=-1, keepdims=True) + NORM_EPS)


def _swiglu(h, w_in, w_out):
    gate, up = jnp.split(h @ w_in, 2, axis=-1)
    return (jax.nn.silu(gate) * up) @ w_out


def _chunk_len(t, c):
    return c if t % c == 0 else t


def _to_chunks(a, c):
    b, t = a.shape[:2]
    a = a.reshape((b, t // c, c) + a.shape[2:])
    return jnp.swapaxes(jnp.moveaxis(a, 1, 0), 2, 3)


def _from_chunks(a):
    a = jnp.moveaxis(jnp.swapaxes(a, 2, 3), 0, 1)
    return a.reshape((a.shape[0], a.shape[1] * a.shape[2]) + a.shape[3:])


def _fox_project(h, w_in, b_f, q_norm, k_norm):
    b, t, _ = h.shape
    hd = (b, t, FOX_HEADS, FOX_HEAD_DIM)
    q, k, v, f = jnp.split(h @ w_in, [D_MODEL, 2 * D_MODEL, 3 * D_MODEL], axis=-1)
    q = _rms_norm(q.reshape(hd), q_norm)
    k = _rms_norm(k.reshape(hd), k_norm)
    log_f = jax.nn.log_sigmoid((f + b_f).astype(jnp.float32))
    return q, k, v.reshape(hd), log_f


def _fox_prompt(h, w_in, b_f, q_norm, k_norm, w_o):
    q, k, v, log_f = _fox_project(h, w_in, b_f, q_norm, k_norm)
    b, s = h.shape[:2]
    nb = s // FOX_Q_BLOCK
    scale = FOX_HEAD_DIM ** -0.5
    c_keys = jnp.swapaxes(jnp.cumsum(log_f, axis=1), 1, 2)
    q_blocks = jnp.moveaxis(q.reshape(b, nb, FOX_Q_BLOCK, FOX_HEADS, FOX_HEAD_DIM), 1, 0)
    c_blocks = jnp.moveaxis(c_keys.reshape(b, FOX_HEADS, nb, FOX_Q_BLOCK), 2, 0)
    k_pos = jnp.arange(s)

    def block(args):
        qb, cb, blk = args
        logits = jnp.einsum('bqhd,bkhd->bhqk', qb, k).astype(jnp.float32) * scale
        logits = logits + cb[..., None] - c_keys[:, :, None, :]
        q_pos = blk * FOX_Q_BLOCK + jnp.arange(FOX_Q_BLOCK)
        causal = k_pos[None, :] <= q_pos[:, None]
        p = jax.nn.softmax(jnp.where(causal, logits, -jnp.inf), axis=-1)
        return jnp.einsum('bhqk,bkhd->bqhd', p.astype(v.dtype), v)

    o = lax.map(block, (q_blocks, c_blocks, jnp.arange(nb)))
    o = jnp.moveaxis(o, 0, 1).reshape(b, s, D_MODEL)
    return o @ w_o, (k, v, log_f)


def _fox_sample(h, cache_k, cache_v, cache_logf, page_table, w_in, b_f, q_norm, k_norm, w_o):
    q, k, v, log_f = _fox_project(h, w_in, b_f, q_norm, k_norm)
    b, t = h.shape[:2]
    past = page_table.shape[1] * PAGE_SIZE
    scale = FOX_HEAD_DIM ** -0.5
    k_past = cache_k[page_table].reshape(b, past, FOX_HEADS, FOX_HEAD_DIM)
    v_past = cache_v[page_table].reshape(b, past, FOX_HEADS, FOX_HEAD_DIM)
    lf_past = cache_logf[page_table].reshape(b, past, FOX_HEADS).astype(jnp.float32)
    after_past = jnp.swapaxes(lax.cumsum(lf_past, axis=1, reverse=True) - lf_past, 1, 2)
    c_new = jnp.swapaxes(jnp.cumsum(log_f, axis=1), 1, 2)
    logit_past = (jnp.einsum('bthd,bphd->bhtp', q, k_past).astype(jnp.float32) * scale
                  + after_past[:, :, None, :] + c_new[..., None])
    logit_new = (jnp.einsum('bthd,bmhd->bhtm', q, k).astype(jnp.float32) * scale
                 + c_new[..., None] - c_new[:, :, None, :])
    causal = jnp.arange(t)[None, :] <= jnp.arange(t)[:, None]
    logit_new = jnp.where(causal, logit_new, -jnp.inf)
    p = jax.nn.softmax(jnp.concatenate([logit_past, logit_new], axis=-1), axis=-1).astype(v.dtype)
    o = (jnp.einsum('bhtp,bphd->bthd', p[..., :past], v_past)
         + jnp.einsum('bhtm,bmhd->bthd', p[..., past:], v))
    return o.reshape(b, t, D_MODEL) @ w_o, (k, v, log_f)


def _gated_delta_chunked(q, k, v, beta, g, s0):
    t = q.shape[1]
    dv = v.shape[-1]
    c = _chunk_len(t, GDN_CHUNK)
    tril = jnp.tril(jnp.ones((c, c), bool))
    strict = jnp.tril(jnp.ones((c, c), bool), -1)
    eye = jnp.eye(c, dtype=jnp.float32)

    def step(s, xs):
        qc, kc, vc, bc, gc = xs
        G = jnp.cumsum(gc, axis=-1)
        diff = G[..., :, None] - G[..., None, :]
        decay = jnp.where(tril, jnp.exp(jnp.where(tril, diff, 0.0)), 0.0)
        kb = kc * bc[..., None]
        lower = jnp.where(strict, jnp.einsum('bhid,bhjd->bhij', kb, kc) * decay, 0.0)
        rhs = jnp.concatenate([vc * bc[..., None], kb * jnp.exp(G)[..., None]], axis=-1)
        sol = lax.linalg.triangular_solve(eye + lower, rhs, left_side=True, lower=True)
        u, w = sol[..., :dv], sol[..., dv:]
        v_new = u - jnp.einsum('bhcd,bhde->bhce', w, s)
        attn = jnp.where(tril, jnp.einsum('bhid,bhjd->bhij', qc, kc) * decay, 0.0)
        o = (jnp.einsum('bhcd,bhde->bhce', qc * jnp.exp(G)[..., None], s)
             + jnp.einsum('bhij,bhje->bhie', attn, v_new))
        g_last = G[..., -1]
        s = (s * jnp.exp(g_last)[..., None, None]
             + jnp.einsum('bhcd,bhce->bhde', kc * jnp.exp(g_last[..., None] - G)[..., None], v_new))
        return s, o

    xs = tuple(_to_chunks(a, c) for a in (q, k, v, beta, g))
    s_final, o = lax.scan(step, s0, xs)
    return _from_chunks(o), s_final


def _gdn_mixer(h, conv_buf, s0, w_in, conv_w, a_log, dt_bias, o_norm, w_o):
    b, t, _ = h.shape
    qkv, z, b_in, a_in = jnp.split(
        h @ w_in, [GDN_CONV_DIM, GDN_CONV_DIM + GDN_V_DIM, GDN_CONV_DIM + GDN_V_DIM + GDN_V_HEADS], axis=-1)
    xc = jnp.concatenate([conv_buf.astype(qkv.dtype), qkv], axis=1)
    conv = xc[:, 0:t] * conv_w[0]
    for w in range(1, GDN_CONV_WIDTH):
        conv = conv + xc[:, w:w + t] * conv_w[w]
    conv = jax.nn.silu(conv)
    new_buf = xc[:, t:]
    q, k, v = jnp.split(conv, [GDN_QK_DIM, 2 * GDN_QK_DIM], axis=-1)
    rep = GDN_V_HEADS // GDN_QK_HEADS
    q = jnp.repeat(_l2_norm(q.reshape(b, t, GDN_QK_HEADS, GDN_DK)) * GDN_DK ** -0.5, rep, axis=2)
    k = jnp.repeat(_l2_norm(k.reshape(b, t, GDN_QK_HEADS, GDN_DK)), rep, axis=2)
    v = v.reshape(b, t, GDN_V_HEADS, GDN_DV).astype(jnp.float32)
    beta = jax.nn.sigmoid(b_in.astype(jnp.float32))
    g = -jnp.exp(a_log.astype(jnp.float32)) * jax.nn.softplus(a_in.astype(jnp.float32) + dt_bias)
    o, s_new = _gated_delta_chunked(q, k, v, beta, g, s0.astype(jnp.float32))
    o = _rms_norm(o, o_norm) * jax.nn.silu(z.reshape(b, t, GDN_V_HEADS, GDN_DV).astype(jnp.float32))
    y = o.reshape(b, t, GDN_V_DIM).astype(h.dtype) @ w_o
    return y, (new_buf, s_new)


def _rotary(x, pos):
    half = x.shape[-1] // 2
    inv = ROPE_BASE ** (-jnp.arange(half, dtype=jnp.float32) / half)
    ang = pos.astype(jnp.float32)[:, None] * inv[None, :]
    cos = jnp.cos(ang)[None, :, None, :]
    sin = jnp.sin(ang)[None, :, None, :]
    x1, x2 = x[..., :half], x[..., half:]
    return jnp.concatenate([x1 * cos - x2 * sin, x1 * sin + x2 * cos], axis=-1)


def _retention_chunked(q, k, v, s0):
    t = q.shape[1]
    c = _chunk_len(t, RET_CHUNK)
    log_gamma = jnp.log1p(-jnp.exp2(-5.0 - jnp.arange(RET_HEADS, dtype=jnp.float32)))
    j = jnp.arange(c, dtype=jnp.float32)
    diff = j[:, None] - j[None, :]
    tril = diff >= 0
    decay = jnp.where(tril, jnp.exp(log_gamma[:, None, None] * jnp.where(tril, diff, 0.0)), 0.0)
    q_decay = jnp.exp(log_gamma[:, None] * (j + 1.0))
    k_decay = jnp.exp(log_gamma[:, None] * (c - 1.0 - j))
    chunk_decay = jnp.exp(log_gamma * c)

    def step(s, xs):
        qc, kc, vc = xs
        inner = jnp.einsum('bhij,bhje->bhie', jnp.einsum('bhid,bhjd->bhij', qc, kc) * decay, vc)
        cross = jnp.einsum('bhcd,bhde->bhce', qc * q_decay[..., None], s)
        s = s * chunk_decay[:, None, None] + jnp.einsum('bhcd,bhce->bhde', kc * k_decay[..., None], vc)
        return s, inner + cross

    xs = tuple(_to_chunks(a, c) for a in (q, k, v))
    s_final, o = lax.scan(step, s0, xs)
    return _from_chunks(o), s_final


def _ret_mixer(h, s0, pos0, w_in, gn_gain, w_o):
    b, t, _ = h.shape
    q, k, v, gate = jnp.split(h @ w_in, [D_MODEL, 2 * D_MODEL, 2 * D_MODEL + RET_V_DIM], axis=-1)
    pos = pos0 + jnp.arange(t)
    q = _rotary(q.reshape(b, t, RET_HEADS, RET_DK).astype(jnp.float32), pos)
    k = _rotary(k.reshape(b, t, RET_HEADS, RET_DK).astype(jnp.float32), pos) * RET_DK ** -0.5
    v = v.reshape(b, t, RET_HEADS, RET_DV).astype(jnp.float32)
    o, s_new = _retention_chunked(q, k, v, s0.astype(jnp.float32))
    mu = jnp.mean(o, axis=-1, keepdims=True)
    var = jnp.mean(jnp.square(o - mu), axis=-1, keepdims=True)
    o = ((o - mu) * lax.rsqrt(var + GN_EPS)).reshape(b, t, RET_V_DIM) * gn_gain.astype(jnp.float32)
    y = (jax.nn.silu(gate.astype(jnp.float32)) * o).astype(h.dtype) @ w_o
    return y, (s_new,)


def _macaron(x, i, norm_g, ffn_w_in, ffn_w_out, mixer):
    x = x + FFN_HALF * _swiglu(_rms_norm(x, norm_g[i, 0]), ffn_w_in[i, 0], ffn_w_out[i, 0])
    y, state = mixer(_rms_norm(x, norm_g[i, 1]))
    x = x + y
    x = x + FFN_HALF * _swiglu(_rms_norm(x, norm_g[i, 2]), ffn_w_in[i, 1], ffn_w_out[i, 1])
    return x, state


def setup_inputs(seed: int = 0) -> dict:
    key = jax.random.key(seed)
    keys = iter(jax.random.split(key, 48))

    def normal(shape, scale=1.0):
        return scale * jax.random.normal(next(keys), shape, jnp.float32)

    def gain(n):
        return 1.0 + normal((n,), 0.02)

    n_pages = PAST_LEN // PAGE_SIZE
    n_used = DEC_BATCH * n_pages
    n_pool = n_used + max(1, n_used // 4)
    kv_shape = (n_pool, PAGE_SIZE, FOX_HEADS, FOX_HEAD_DIM)
    lf_shape = (n_pool, PAGE_SIZE, FOX_HEADS)
    fox_bias = jnp.linspace(FOX_BIAS_LO, FOX_BIAS_HI, FOX_HEADS, dtype=jnp.float32)

    x_prompt = normal((BATCH, SEQ, D_MODEL))
    x_sample = normal((DEC_BATCH, DEC_SEQ, D_MODEL))
    cache_k_l0 = normal(kv_shape)
    cache_v_l0 = normal(kv_shape)
    cache_logf_l0 = jax.nn.log_sigmoid(fox_bias + normal(lf_shape, 0.5))
    state_conv_l1 = normal((DEC_BATCH, GDN_CONV_WIDTH - 1, GDN_CONV_DIM))
    state_gdn_l1 = normal((DEC_BATCH, GDN_V_HEADS, GDN_DK, GDN_DV), 0.1)
    state_ret_l2 = normal((DEC_BATCH, RET_HEADS, RET_DK, RET_DV), 0.5)
    cache_k_l3 = normal(kv_shape)
    cache_v_l3 = normal(kv_shape)
    cache_logf_l3 = jax.nn.log_sigmoid(fox_bias + normal(lf_shape, 0.5))
    page_table = jax.random.permutation(next(keys), n_pool)[:n_used].reshape(DEC_BATCH, n_pages).astype(jnp.int32)

    norm_g = 1.0 + normal((DEPTH, 3, D_MODEL), 0.02)
    ffn_w_in = normal((DEPTH, 2, D_MODEL, 2 * D_FF), D_MODEL ** -0.5)
    ffn_w_out = normal((DEPTH, 2, D_FF, D_MODEL), D_FF ** -0.5)

    fox_in = 3 * D_MODEL + FOX_HEADS
    fox_w_in_l0 = normal((D_MODEL, fox_in), D_MODEL ** -0.5)
    fox_b_f_l0 = fox_bias + normal((FOX_HEADS,), 0.1)
    fox_q_norm_l0 = gain(FOX_HEAD_DIM)
    fox_k_norm_l0 = gain(FOX_HEAD_DIM)
    fox_w_o_l0 = normal((D_MODEL, D_MODEL), D_MODEL ** -0.5)

    gdn_w_in_l1 = normal((D_MODEL, GDN_CONV_DIM + GDN_V_DIM + 2 * GDN_V_HEADS), D_MODEL ** -0.5)
    gdn_conv_w_l1 = normal((GDN_CONV_WIDTH, GDN_CONV_DIM), GDN_CONV_WIDTH ** -0.5)
    gdn_a_log_l1 = jnp.log(jax.random.uniform(next(keys), (GDN_V_HEADS,), jnp.float32, 1.0, 16.0))
    gdn_dt_bias_l1 = normal((GDN_V_HEADS,), 0.1)
    gdn_o_norm_l1 = gain(GDN_DV)
    gdn_w_o_l1 = normal((GDN_V_DIM, D_MODEL), GDN_V_DIM ** -0.5)

    ret_w_in_l2 = normal((D_MODEL, 2 * D_MODEL + 2 * RET_V_DIM), D_MODEL ** -0.5)
    ret_gn_l2 = gain(RET_V_DIM)
    ret_w_o_l2 = normal((RET_V_DIM, D_MODEL), RET_V_DIM ** -0.5)

    fox_w_in_l3 = normal((D_MODEL, fox_in), D_MODEL ** -0.5)
    fox_b_f_l3 = fox_bias + normal((FOX_HEADS,), 0.1)
    fox_q_norm_l3 = gain(FOX_HEAD_DIM)
    fox_k_norm_l3 = gain(FOX_HEAD_DIM)
    fox_w_o_l3 = normal((D_MODEL, D_MODEL), D_MODEL ** -0.5)

    return {
        'x_prompt': x_prompt, 'x_sample': x_sample,
        'cache_k_l0': cache_k_l0, 'cache_v_l0': cache_v_l0, 'cache_logf_l0': cache_logf_l0,
        'state_conv_l1': state_conv_l1, 'state_gdn_l1': state_gdn_l1,
        'state_ret_l2': state_ret_l2,
        'cache_k_l3': cache_k_l3, 'cache_v_l3': cache_v_l3, 'cache_logf_l3': cache_logf_l3,
        'page_table': page_table,
        'norm_g': norm_g, 'ffn_w_in': ffn_w_in, 'ffn_w_out': ffn_w_out,
        'fox_w_in_l0': fox_w_in_l0, 'fox_b_f_l0': fox_b_f_l0, 'fox_q_norm_l0': fox_q_norm_l0,
        'fox_k_norm_l0': fox_k_norm_l0, 'fox_w_o_l0': fox_w_o_l0,
        'gdn_w_in_l1': gdn_w_in_l1, 'gdn_conv_w_l1': gdn_conv_w_l1, 'gdn_a_log_l1': gdn_a_log_l1,
        'gdn_dt_bias_l1': gdn_dt_bias_l1, 'gdn_o_norm_l1': gdn_o_norm_l1, 'gdn_w_o_l1': gdn_w_o_l1,
        'ret_w_in_l2': ret_w_in_l2, 'ret_gn_l2': ret_gn_l2, 'ret_w_o_l2': ret_w_o_l2,
        'fox_w_in_l3': fox_w_in_l3, 'fox_b_f_l3': fox_b_f_l3, 'fox_q_norm_l3': fox_q_norm_l3,
        'fox_k_norm_l3': fox_k_norm_l3, 'fox_w_o_l3': fox_w_o_l3,
    }


def reference(x_prompt, x_sample, cache_k_l0, cache_v_l0, cache_logf_l0, state_conv_l1, state_gdn_l1,
              state_ret_l2, cache_k_l3, cache_v_l3, cache_logf_l3, page_table, norm_g, ffn_w_in, ffn_w_out,
              fox_w_in_l0, fox_b_f_l0, fox_q_norm_l0, fox_k_norm_l0, fox_w_o_l0,
              gdn_w_in_l1, gdn_conv_w_l1, gdn_a_log_l1, gdn_dt_bias_l1, gdn_o_norm_l1, gdn_w_o_l1,
              ret_w_in_l2, ret_gn_l2, ret_w_o_l2,
              fox_w_in_l3, fox_b_f_l3, fox_q_norm_l3, fox_k_norm_l3, fox_w_o_l3):
    layers = (
        ('fox', (fox_w_in_l0, fox_b_f_l0, fox_q_norm_l0, fox_k_norm_l0, fox_w_o_l0),
         (cache_k_l0, cache_v_l0, cache_logf_l0)),
        ('gdn', (gdn_w_in_l1, gdn_conv_w_l1, gdn_a_log_l1, gdn_dt_bias_l1, gdn_o_norm_l1, gdn_w_o_l1),
         (state_conv_l1, state_gdn_l1)),
        ('ret', (ret_w_in_l2, ret_gn_l2, ret_w_o_l2), (state_ret_l2,)),
        ('fox', (fox_w_in_l3, fox_b_f_l3, fox_q_norm_l3, fox_k_norm_l3, fox_w_o_l3),
         (cache_k_l3, cache_v_l3, cache_logf_l3)),
    )
    xp, xs = x_prompt, x_sample
    new_state = []
    for i in range(DEPTH):
        kind, params, cache = layers[i]
        if kind == 'fox':
            def mix_p(h):
                return _fox_prompt(h, *params)

            def mix_s(h):
                return _fox_sample(h, *cache, page_table, *params)
        elif kind == 'gdn':
            def mix_p(h):
                conv0 = jnp.zeros((h.shape[0], GDN_CONV_WIDTH - 1, GDN_CONV_DIM), h.dtype)
                s0 = jnp.zeros((h.shape[0], GDN_V_HEADS, GDN_DK, GDN_DV), jnp.float32)
                return _gdn_mixer(h, conv0, s0, *params)

            def mix_s(h):
                return _gdn_mixer(h, *cache, *params)
        else:
            def mix_p(h):
                s0 = jnp.zeros((h.shape[0], RET_HEADS, RET_DK, RET_DV), jnp.float32)
                return _ret_mixer(h, s0, 0, *params)

            def mix_s(h):
                return _ret_mixer(h, *cache, PAST_LEN, *params)
        xp, st_p = _macaron(xp, i, norm_g, ffn_w_in, ffn_w_out, mix_p)
        xs, st_s = _macaron(xs, i, norm_g, ffn_w_in, ffn_w_out, mix_s)
        new_state += [*st_p, *st_s]
    return (xp, xs, *new_state)
```

```python
import functools

import jax
import jax.numpy as jnp
from jax import lax
from jax.experimental import pallas as pl
from jax.experimental.pallas import tpu as pltpu

F32 = jnp.float32
BF16 = jnp.bfloat16

D_MODEL = 2048
DEPTH = 4
PAST_LEN = 16384
PAGE_SIZE = 128
D_FF = 5632
NORM_EPS = 1e-6
GN_EPS = 1e-5
FFN_HALF = 0.5

FOX_HEADS = 16
FOX_HEAD_DIM = D_MODEL // FOX_HEADS
FOX_Q_BLOCK = 128

GDN_QK_HEADS = 16
GDN_V_HEADS = 32
GDN_DK = 128
GDN_DV = 128
GDN_QK_DIM = GDN_QK_HEADS * GDN_DK
GDN_V_DIM = GDN_V_HEADS * GDN_DV
GDN_CONV_DIM = 2 * GDN_QK_DIM + GDN_V_DIM
GDN_CONV_WIDTH = 4
GDN_CHUNK = 64

RET_HEADS = 8
RET_DK = D_MODEL // RET_HEADS
RET_DV = 2 * RET_DK
RET_V_DIM = RET_HEADS * RET_DV
RET_CHUNK = 128
ROPE_BASE = 10000.0

LANES = 128
VMEM_LIMIT_BYTES = 56 * 1024 * 1024
ROW_TILE = 512
COL_TILE = 1024
FF_TILE = 512


def _row_tile(m):
    return ROW_TILE if m % ROW_TILE == 0 else m


def _col_tile(n):
    for t in (COL_TILE, 768, 512, 256, 128):
        if n % t == 0:
            return t
    return n


def _ffn_kernel(x_ref, g_ref, wg_ref, wu_ref, wo_ref, o_ref, xn_ref):
    j = pl.program_id(1)

    @pl.when(j == 0)
    def _():
        x = x_ref[...]
        inv = lax.rsqrt(jnp.mean(x * x, axis=-1, keepdims=True) + NORM_EPS)
        xn_ref[...] = (x * inv * g_ref[...]).astype(BF16)
        o_ref[...] = jnp.zeros_like(o_ref)

    xn = xn_ref[...]
    gate = jnp.dot(xn, wg_ref[...], preferred_element_type=F32)
    up = jnp.dot(xn, wu_ref[...], preferred_element_type=F32)
    act = (gate * jax.nn.sigmoid(gate) * up).astype(BF16)
    o_ref[...] += jnp.dot(act, wo_ref[...], preferred_element_type=F32)

    @pl.when(j == pl.num_programs(1) - 1)
    def _():
        o_ref[...] = x_ref[...] + FFN_HALF * o_ref[...]


def _ffn(x, g, w_in, w_out, layer, which):
    m, d = x.shape
    f = w_out.shape[2]
    tm = _row_tile(m)
    tf = FF_TILE
    nf = f // tf
    return pl.pallas_call(
        _ffn_kernel,
        grid=(m // tm, nf),
        in_specs=[
            pl.BlockSpec((tm, d), lambda i, j: (i, 0)),
            pl.BlockSpec((1, d), lambda i, j: (0, 0)),
            pl.BlockSpec((None, None, d, tf), lambda i, j: (layer, which, 0, j)),
            pl.BlockSpec((None, None, d, tf), lambda i, j: (layer, which, 0, j + nf)),
            pl.BlockSpec((None, None, tf, d), lambda i, j: (layer, which, j, 0)),
        ],
        out_specs=pl.BlockSpec((tm, d), lambda i, j: (i, 0)),
        out_shape=jax.ShapeDtypeStruct((m, d), F32),
        scratch_shapes=[pltpu.VMEM((tm, d), BF16)],
        compiler_params=pltpu.CompilerParams(
            dimension_semantics=("parallel", "arbitrary"), vmem_limit_bytes=VMEM_LIMIT_BYTES),
        name="ffn",
    )(x, g.reshape(1, d), w_in, w_in, w_out)


def _proj_kernel(*refs, norm, residual):
    refs = list(refs)
    x_ref = refs.pop(0)
    g_ref = refs.pop(0) if norm else None
    w_ref = refs.pop(0)
    r_ref = refs.pop(0) if residual else None
    o_ref = refs.pop(0)
    if norm:
        xn_ref = refs.pop(0)

        @pl.when(pl.program_id(1) == 0)
        def _():
            x = x_ref[...]
            inv = lax.rsqrt(jnp.mean(x * x, axis=-1, keepdims=True) + NORM_EPS)
            xn_ref[...] = (x * inv * g_ref[...]).astype(BF16)

        lhs = xn_ref[...]
    else:
        lhs = x_ref[...]
    y = jnp.dot(lhs, w_ref[...], preferred_element_type=F32)
    if residual:
        y = r_ref[...] + y
    o_ref[...] = y


def _proj(x, w, gain=None, residual=None):
    m, k = x.shape
    n = w.shape[1]
    tm = _row_tile(m)
    tn = _col_tile(n)
    norm = gain is not None
    in_specs = [pl.BlockSpec((tm, k), lambda i, j: (i, 0))]
    args = [x]
    if norm:
        in_specs.append(pl.BlockSpec((1, k), lambda i, j: (0, 0)))
        args.append(gain.reshape(1, k))
    in_specs.append(pl.BlockSpec((k, tn), lambda i, j: (0, j)))
    args.append(w)
    if residual is not None:
        in_specs.append(pl.BlockSpec((tm, tn), lambda i, j: (i, j)))
        args.append(residual)
    return pl.pallas_call(
        functools.partial(_proj_kernel, norm=norm, residual=residual is not None),
        grid=(m // tm, n // tn),
        in_specs=in_specs,
        out_specs=pl.BlockSpec((tm, tn), lambda i, j: (i, j)),
        out_shape=jax.ShapeDtypeStruct((m, n), F32),
        scratch_shapes=[pltpu.VMEM((tm, k), BF16)] if norm else [],
        compiler_params=pltpu.CompilerParams(
            dimension_semantics=("parallel", "arbitrary"), vmem_limit_bytes=VMEM_LIMIT_BYTES),
        name="proj",
    )(*args)


def _pad_cols(w):
    return jnp.pad(w, ((0, 0), (0, LANES - w.shape[1])))


def _rms_norm(x, g):
    xf = x.astype(F32)
    y = xf * lax.rsqrt(jnp.mean(xf * xf, axis=-1, keepdims=True) + NORM_EPS)
    return (y * g.astype(F32)).astype(x.dtype)


def _l2_norm(x):
    xf = x.astype(F32)
    return xf * lax.rsqrt(jnp.sum(xf * xf, axis=-1, keepdims=True) + NORM_EPS)


def _chunk_len(t, c):
    return c if t % c == 0 else t


def _to_chunks(a, c):
    b, t = a.shape[:2]
    a = a.reshape((b, t // c, c) + a.shape[2:])
    return jnp.swapaxes(jnp.moveaxis(a, 1, 0), 2, 3)


def _from_chunks(a):
    a = jnp.moveaxis(jnp.swapaxes(a, 2, 3), 0, 1)
    return a.reshape((a.shape[0], a.shape[1] * a.shape[2]) + a.shape[3:])


def _fox_prompt_attn(q, k, v, log_f):
    b, s = q.shape[:2]
    nb = s // FOX_Q_BLOCK
    scale = FOX_HEAD_DIM ** -0.5
    c_keys = jnp.swapaxes(jnp.cumsum(log_f, axis=1), 1, 2)
    q_blocks = jnp.moveaxis(q.reshape(b, nb, FOX_Q_BLOCK, FOX_HEADS, FOX_HEAD_DIM), 1, 0)
    c_blocks = jnp.moveaxis(c_keys.reshape(b, FOX_HEADS, nb, FOX_Q_BLOCK), 2, 0)
    k_pos = jnp.arange(s)

    def block(args):
        qb, cb, blk = args
        logits = jnp.einsum('bqhd,bkhd->bhqk', qb, k).astype(F32) * scale
        logits = logits + cb[..., None] - c_keys[:, :, None, :]
        q_pos = blk * FOX_Q_BLOCK + jnp.arange(FOX_Q_BLOCK)
        causal = k_pos[None, :] <= q_pos[:, None]
        p = jax.nn.softmax(jnp.where(causal, logits, -jnp.inf), axis=-1)
        return jnp.einsum('bhqk,bkhd->bqhd', p.astype(v.dtype), v)

    o = lax.map(block, (q_blocks, c_blocks, jnp.arange(nb)))
    return jnp.moveaxis(o, 0, 1).reshape(b, s, D_MODEL)


def _fox_sample_attn(q, k, v, log_f, cache_k, cache_v, cache_logf, page_table):
    b, t = q.shape[:2]
    past = page_table.shape[1] * PAGE_SIZE
    scale = FOX_HEAD_DIM ** -0.5
    k_past = cache_k[page_table].reshape(b, past, FOX_HEADS, FOX_HEAD_DIM)
    v_past = cache_v[page_table].reshape(b, past, FOX_HEADS, FOX_HEAD_DIM)
    lf_past = cache_logf[page_table].reshape(b, past, FOX_HEADS).astype(F32)
    after_past = jnp.swapaxes(lax.cumsum(lf_past, axis=1, reverse=True) - lf_past, 1, 2)
    c_new = jnp.swapaxes(jnp.cumsum(log_f, axis=1), 1, 2)
    logit_past = (jnp.einsum('bthd,bphd->bhtp', q, k_past).astype(F32) * scale
                  + after_past[:, :, None, :] + c_new[..., None])
    logit_new = (jnp.einsum('bthd,bmhd->bhtm', q, k).astype(F32) * scale
                 + c_new[..., None] - c_new[:, :, None, :])
    causal = jnp.arange(t)[None, :] <= jnp.arange(t)[:, None]
    logit_new = jnp.where(causal, logit_new, -jnp.inf)
    p = jax.nn.softmax(jnp.concatenate([logit_past, logit_new], axis=-1), axis=-1).astype(v.dtype)
    o = (jnp.einsum('bhtp,bphd->bthd', p[..., :past], v_past)
         + jnp.einsum('bhtm,bmhd->bthd', p[..., past:], v))
    return o.reshape(b, t, D_MODEL)


def _gated_delta_chunked(q, k, v, beta, g, s0):
    t = q.shape[1]
    dv = v.shape[-1]
    c = _chunk_len(t, GDN_CHUNK)
    tril = jnp.tril(jnp.ones((c, c), bool))
    strict = jnp.tril(jnp.ones((c, c), bool), -1)
    eye = jnp.eye(c, dtype=F32)

    def step(s, xs):
        qc, kc, vc, bc, gc = xs
        G = jnp.cumsum(gc, axis=-1)
        diff = G[..., :, None] - G[..., None, :]
        decay = jnp.where(tril, jnp.exp(jnp.where(tril, diff, 0.0)), 0.0)
        kb = kc * bc[..., None]
        lower = jnp.where(strict, jnp.einsum('bhid,bhjd->bhij', kb, kc) * decay, 0.0)
        rhs = jnp.concatenate([vc * bc[..., None], kb * jnp.exp(G)[..., None]], axis=-1)
        sol = lax.linalg.triangular_solve(eye + lower, rhs, left_side=True, lower=True)
        u, w = sol[..., :dv], sol[..., dv:]
        v_new = u - jnp.einsum('bhcd,bhde->bhce', w, s)
        attn = jnp.where(tril, jnp.einsum('bhid,bhjd->bhij', qc, kc) * decay, 0.0)
        o = (jnp.einsum('bhcd,bhde->bhce', qc * jnp.exp(G)[..., None], s)
             + jnp.einsum('bhij,bhje->bhie', attn, v_new))
        g_last = G[..., -1]
        s = (s * jnp.exp(g_last)[..., None, None]
             + jnp.einsum('bhcd,bhce->bhde', kc * jnp.exp(g_last[..., None] - G)[..., None], v_new))
        return s, o

    xs = tuple(_to_chunks(a, c) for a in (q, k, v, beta, g))
    s_final, o = lax.scan(step, s0, xs)
    return _from_chunks(o), s_final


def _rotary(x, pos):
    half = x.shape[-1] // 2
    inv = ROPE_BASE ** (-jnp.arange(half, dtype=F32) / half)
    ang = pos.astype(F32)[:, None] * inv[None, :]
    cos = jnp.cos(ang)[None, :, None, :]
    sin = jnp.sin(ang)[None, :, None, :]
    x1, x2 = x[..., :half], x[..., half:]
    return jnp.concatenate([x1 * cos - x2 * sin, x1 * sin + x2 * cos], axis=-1)


def _retention_chunked(q, k, v, s0):
    t = q.shape[1]
    c = _chunk_len(t, RET_CHUNK)
    log_gamma = jnp.log1p(-jnp.exp2(-5.0 - jnp.arange(RET_HEADS, dtype=F32)))
    j = jnp.arange(c, dtype=F32)
    diff = j[:, None] - j[None, :]
    tril = diff >= 0
    decay = jnp.where(tril, jnp.exp(log_gamma[:, None, None] * jnp.where(tril, diff, 0.0)), 0.0)
    q_decay = jnp.exp(log_gamma[:, None] * (j + 1.0))
    k_decay = jnp.exp(log_gamma[:, None] * (c - 1.0 - j))
    chunk_decay = jnp.exp(log_gamma * c)

    def step(s, xs):
        qc, kc, vc = xs
        inner = jnp.einsum('bhij,bhje->bhie', jnp.einsum('bhid,bhjd->bhij', qc, kc) * decay, vc)
        cross = jnp.einsum('bhcd,bhde->bhce', qc * q_decay[..., None], s)
        s = s * chunk_decay[:, None, None] + jnp.einsum('bhcd,bhce->bhde', kc * k_decay[..., None], vc)
        return s, inner + cross

    xs = tuple(_to_chunks(a, c) for a in (q, k, v))
    s_final, o = lax.scan(step, s0, xs)
    return _from_chunks(o), s_final


def _fox_mixer(x, b, t, g, w_in, b_f, q_norm, k_norm, w_o, cache):
    hd = (b, t, FOX_HEADS, FOX_HEAD_DIM)
    qkv = _proj(x, w_in[:, :3 * D_MODEL], gain=g)
    f = _proj(x, _pad_cols(w_in[:, 3 * D_MODEL:]), gain=g)[:, :FOX_HEADS]
    q = _rms_norm(qkv[:, :D_MODEL].reshape(hd), q_norm)
    k = _rms_norm(qkv[:, D_MODEL:2 * D_MODEL].reshape(hd), k_norm)
    v = qkv[:, 2 * D_MODEL:].reshape(hd)
    log_f = jax.nn.log_sigmoid(f.reshape(b, t, FOX_HEADS) + b_f)
    if cache is None:
        o = _fox_prompt_attn(q, k, v, log_f)
    else:
        o = _fox_sample_attn(q, k, v, log_f, *cache)
    y = _proj(o.reshape(b * t, D_MODEL).astype(BF16), w_o, residual=x)
    return y, (k, v, log_f)


def _gdn_mixer(x, b, t, g, w_in, conv_w, a_log, dt_bias, o_norm, w_o, conv_buf, s0):
    n_main = GDN_CONV_DIM + GDN_V_DIM
    main = _proj(x, w_in[:, :n_main], gain=g)
    tail = _proj(x, _pad_cols(w_in[:, n_main:]), gain=g)
    qkv = main[:, :GDN_CONV_DIM].reshape(b, t, GDN_CONV_DIM)
    z = main[:, GDN_CONV_DIM:].reshape(b, t, GDN_V_HEADS, GDN_DV)
    b_in = tail[:, :GDN_V_HEADS].reshape(b, t, GDN_V_HEADS)
    a_in = tail[:, GDN_V_HEADS:2 * GDN_V_HEADS].reshape(b, t, GDN_V_HEADS)
    xc = jnp.concatenate([conv_buf, qkv], axis=1)
    conv = xc[:, 0:t] * conv_w[0]
    for w in range(1, GDN_CONV_WIDTH):
        conv = conv + xc[:, w:w + t] * conv_w[w]
    conv = jax.nn.silu(conv)
    new_buf = xc[:, t:]
    q, k, v = jnp.split(conv, [GDN_QK_DIM, 2 * GDN_QK_DIM], axis=-1)
    rep = GDN_V_HEADS // GDN_QK_HEADS
    q = jnp.repeat(_l2_norm(q.reshape(b, t, GDN_QK_HEADS, GDN_DK)) * GDN_DK ** -0.5, rep, axis=2)
    k = jnp.repeat(_l2_norm(k.reshape(b, t, GDN_QK_HEADS, GDN_DK)), rep, axis=2)
    v = v.reshape(b, t, GDN_V_HEADS, GDN_DV)
    beta = jax.nn.sigmoid(b_in)
    gg = -jnp.exp(a_log) * jax.nn.softplus(a_in + dt_bias)
    o, s_new = _gated_delta_chunked(q, k, v, beta, gg, s0)
    o = _rms_norm(o, o_norm) * jax.nn.silu(z)
    y = _proj(o.reshape(b * t, GDN_V_DIM).astype(BF16), w_o, residual=x)
    return y, (new_buf, s_new)


def _ret_mixer(x, b, t, g, w_in, gn_gain, w_o, s0, pos0):
    proj = _proj(x, w_in, gain=g)
    q = proj[:, :D_MODEL].reshape(b, t, RET_HEADS, RET_DK)
    k = proj[:, D_MODEL:2 * D_MODEL].reshape(b, t, RET_HEADS, RET_DK)
    v = proj[:, 2 * D_MODEL:2 * D_MODEL + RET_V_DIM].reshape(b, t, RET_HEADS, RET_DV)
    gate = proj[:, 2 * D_MODEL + RET_V_DIM:]
    pos = pos0 + jnp.arange(t)
    q = _rotary(q, pos)
    k = _rotary(k, pos) * RET_DK ** -0.5
    o, s_new = _retention_chunked(q, k, v, s0)
    mu = jnp.mean(o, axis=-1, keepdims=True)
    var = jnp.mean(jnp.square(o - mu), axis=-1, keepdims=True)
    o = ((o - mu) * lax.rsqrt(var + GN_EPS)).reshape(b * t, RET_V_DIM) * gn_gain
    y = _proj((jax.nn.silu(gate) * o).astype(BF16), w_o, residual=x)
    return y, (s_new,)


def kernel(x_prompt, x_sample, cache_k_l0, cache_v_l0, cache_logf_l0, state_conv_l1, state_gdn_l1, state_ret_l2, cache_k_l3, cache_v_l3, cache_logf_l3, page_table, norm_g, ffn_w_in, ffn_w_out, fox_w_in_l0, fox_b_f_l0, fox_q_norm_l0, fox_k_norm_l0, fox_w_o_l0, gdn_w_in_l1, gdn_conv_w_l1, gdn_a_log_l1, gdn_dt_bias_l1, gdn_o_norm_l1, gdn_w_o_l1, ret_w_in_l2, ret_gn_l2, ret_w_o_l2, fox_w_in_l3, fox_b_f_l3, fox_q_norm_l3, fox_k_norm_l3, fox_w_o_l3):
    ffn_in = ffn_w_in.astype(BF16)
    ffn_out = ffn_w_out.astype(BF16)
    fox0 = (fox_w_in_l0.astype(BF16), fox_b_f_l0, fox_q_norm_l0, fox_k_norm_l0, fox_w_o_l0.astype(BF16))
    fox3 = (fox_w_in_l3.astype(BF16), fox_b_f_l3, fox_q_norm_l3, fox_k_norm_l3, fox_w_o_l3.astype(BF16))
    gdn = (gdn_w_in_l1.astype(BF16), gdn_conv_w_l1, gdn_a_log_l1, gdn_dt_bias_l1, gdn_o_norm_l1,
           gdn_w_o_l1.astype(BF16))
    ret = (ret_w_in_l2.astype(BF16), ret_gn_l2, ret_w_o_l2.astype(BF16))

    def run(x3, sample):
        b, t, _ = x3.shape
        x = x3.reshape(b * t, D_MODEL)
        states = []
        for i in range(DEPTH):
            x = _ffn(x, norm_g[i, 0], ffn_in, ffn_out, i, 0)
            g = norm_g[i, 1]
            if i == 0:
                cache = (cache_k_l0, cache_v_l0, cache_logf_l0, page_table) if sample else None
                x, st = _fox_mixer(x, b, t, g, *fox0, cache)
            elif i == 1:
                if sample:
                    conv0, s0 = state_conv_l1, state_gdn_l1
                else:
                    conv0 = jnp.zeros((b, GDN_CONV_WIDTH - 1, GDN_CONV_DIM), F32)
                    s0 = jnp.zeros((b, GDN_V_HEADS, GDN_DK, GDN_DV), F32)
                x, st = _gdn_mixer(x, b, t, g, *gdn, conv0, s0)
            elif i == 2:
                s0 = state_ret_l2 if sample else jnp.zeros((b, RET_HEADS, RET_DK, RET_DV), F32)
                x, st = _ret_mixer(x, b, t, g, *ret, s0, PAST_LEN if sample else 0)
            else:
                cache = (cache_k_l3, cache_v_l3, cache_logf_l3, page_table) if sample else None
                x, st = _fox_mixer(x, b, t, g, *fox3, cache)
            x = _ffn(x, norm_g[i, 2], ffn_in, ffn_out, i, 1)
            states.append(st)
        return x.reshape(b, t, D_MODEL), states

    yp, st_p = run(x_prompt, False)
    ys, st_s = run(x_sample, True)
    new_state = []
    for sp, ss in zip(st_p, st_s):
        new_state += [*sp, *ss]
    return (yp, ys, *new_state)
```

```python
import functools

import jax
import jax.numpy as jnp
from jax import lax
from jax.experimental import pallas as pl
from jax.experimental.pallas import tpu as pltpu

F32 = jnp.float32
BF16 = jnp.bfloat16

D_MODEL = 2048
DEPTH = 4
PAST_LEN = 16384
D_FF = 5632
NORM_EPS = 1e-6
GN_EPS = 1e-5
FFN_HALF = 0.5

FOX_HEADS = 16
FOX_HEAD_DIM = D_MODEL // FOX_HEADS

GDN_QK_HEADS = 16
GDN_V_HEADS = 32
GDN_DK = 128
GDN_DV = 128
GDN_QK_DIM = GDN_QK_HEADS * GDN_DK
GDN_V_DIM = GDN_V_HEADS * GDN_DV
GDN_CONV_DIM = 2 * GDN_QK_DIM + GDN_V_DIM
GDN_CONV_WIDTH = 4
GDN_CHUNK = 64

RET_HEADS = 8
RET_DK = D_MODEL // RET_HEADS
RET_DV = 2 * RET_DK
RET_V_DIM = RET_HEADS * RET_DV
RET_CHUNK = 128
ROPE_BASE = 10000.0

LANES = 128
SUBLANES = 8
VMEM_LIMIT_BYTES = 56 * 1024 * 1024
ROW_TILE = 512
COL_TILE = 1024
FF_TILE = 512
FOX_ATTN_TILE = 512
FOX_PAGES_PER_STEP = 4
GDN_HEAD_GROUP = 8


def _row_tile(m):
    return ROW_TILE if m % ROW_TILE == 0 else m


def _col_tile(n):
    for t in (COL_TILE, 768, 512, 256, 128):
        if n % t == 0:
            return t
    return n


def _dot(a, b):
    return jnp.dot(a.astype(BF16), b.astype(BF16), preferred_element_type=F32)


def _dot_nt(a, b):
    return lax.dot_general(a.astype(BF16), b.astype(BF16), (((1,), (1,)), ((), ())), preferred_element_type=F32)


def _dot_tn(a, b):
    return lax.dot_general(a.astype(BF16), b.astype(BF16), (((0,), (0,)), ((), ())), preferred_element_type=F32)


def _dot_f32(a, b):
    return jnp.dot(a, b, preferred_element_type=F32, precision=lax.Precision.HIGHEST)


def _bf(a):
    return a.astype(BF16).astype(F32)


def _ffn_kernel(x_ref, g_ref, wg_ref, wu_ref, wo_ref, o_ref, xn_ref):
    j = pl.program_id(1)

    @pl.when(j == 0)
    def _():
        x = x_ref[...]
        inv = lax.rsqrt(jnp.mean(x * x, axis=-1, keepdims=True) + NORM_EPS)
        xn_ref[...] = (x * inv * g_ref[...]).astype(BF16)
        o_ref[...] = jnp.zeros_like(o_ref)

    xn = xn_ref[...]
    gate = jnp.dot(xn, wg_ref[...], preferred_element_type=F32)
    up = jnp.dot(xn, wu_ref[...], preferred_element_type=F32)
    act = (gate * jax.nn.sigmoid(gate) * up).astype(BF16)
    o_ref[...] += jnp.dot(act, wo_ref[...], preferred_element_type=F32)

    @pl.when(j == pl.num_programs(1) - 1)
    def _():
        o_ref[...] = x_ref[...] + FFN_HALF * o_ref[...]


def _ffn(x, g, w_in, w_out, layer, which):
    m, d = x.shape
    f = w_out.shape[2]
    tm = _row_tile(m)
    tf = FF_TILE
    nf = f // tf
    return pl.pallas_call(
        _ffn_kernel,
        grid=(m // tm, nf),
        in_specs=[
            pl.BlockSpec((tm, d), lambda i, j: (i, 0)),
            pl.BlockSpec((1, d), lambda i, j: (0, 0)),
            pl.BlockSpec((None, None, d, tf), lambda i, j: (layer, which, 0, j)),
            pl.BlockSpec((None, None, d, tf), lambda i, j: (layer, which, 0, j + nf)),
            pl.BlockSpec((None, None, tf, d), lambda i, j: (layer, which, j, 0)),
        ],
        out_specs=pl.BlockSpec((tm, d), lambda i, j: (i, 0)),
        out_shape=jax.ShapeDtypeStruct((m, d), F32),
        scratch_shapes=[pltpu.VMEM((tm, d), BF16)],
        compiler_params=pltpu.CompilerParams(
            dimension_semantics=("parallel", "arbitrary"), vmem_limit_bytes=VMEM_LIMIT_BYTES),
        name="ffn",
    )(x, g.reshape(1, d), w_in, w_in, w_out)


def _proj_kernel(*refs, norm, residual):
    refs = list(refs)
    x_ref = refs.pop(0)
    g_ref = refs.pop(0) if norm else None
    w_ref = refs.pop(0)
    r_ref = refs.pop(0) if residual else None
    o_ref = refs.pop(0)
    if norm:
        xn_ref = refs.pop(0)

        @pl.when(pl.program_id(1) == 0)
        def _():
            x = x_ref[...]
            inv = lax.rsqrt(jnp.mean(x * x, axis=-1, keepdims=True) + NORM_EPS)
            xn_ref[...] = (x * inv * g_ref[...]).astype(BF16)

        lhs = xn_ref[...]
    else:
        lhs = x_ref[...]
    y = jnp.dot(lhs, w_ref[...], preferred_element_type=F32)
    if residual:
        y = r_ref[...] + y
    o_ref[...] = y


def _proj(x, w, gain=None, residual=None):
    m, k = x.shape
    n = w.shape[1]
    tm = _row_tile(m)
    tn = _col_tile(n)
    norm = gain is not None
    in_specs = [pl.BlockSpec((tm, k), lambda i, j: (i, 0))]
    args = [x]
    if norm:
        in_specs.append(pl.BlockSpec((1, k), lambda i, j: (0, 0)))
        args.append(gain.reshape(1, k))
    in_specs.append(pl.BlockSpec((k, tn), lambda i, j: (0, j)))
    args.append(w)
    if residual is not None:
        in_specs.append(pl.BlockSpec((tm, tn), lambda i, j: (i, j)))
        args.append(residual)
    return pl.pallas_call(
        functools.partial(_proj_kernel, norm=norm, residual=residual is not None),
        grid=(m // tm, n // tn),
        in_specs=in_specs,
        out_specs=pl.BlockSpec((tm, tn), lambda i, j: (i, j)),
        out_shape=jax.ShapeDtypeStruct((m, n), F32),
        scratch_shapes=[pltpu.VMEM((tm, k), BF16)] if norm else [],
        compiler_params=pltpu.CompilerParams(
            dimension_semantics=("parallel", "arbitrary"), vmem_limit_bytes=VMEM_LIMIT_BYTES),
        name="proj",
    )(*args)


def _pad_cols(w):
    return jnp.pad(w, ((0, 0), (0, LANES - w.shape[1])))


def _fox_flash_kernel(q_ref, k_ref, v_ref, cq_ref, ck_ref, o_ref, *, tile):
    qi = pl.program_id(2)
    q = q_ref[...]
    cq = cq_ref[...]
    dh = q.shape[1]

    def step(j, carry, diagonal):
        m, l, acc = carry
        ks = pl.multiple_of(j * tile, tile)
        kb = k_ref[pl.ds(ks, tile), :]
        vb = v_ref[pl.ds(ks, tile), :]
        ck = ck_ref[:, pl.ds(ks, tile)]
        s = lax.dot_general(q, kb, (((1,), (1,)), ((), ())), preferred_element_type=F32)
        s = s + (cq - ck)
        if diagonal:
            row = lax.broadcasted_iota(jnp.int32, s.shape, 0)
            col = lax.broadcasted_iota(jnp.int32, s.shape, 1)
            s = jnp.where(col <= row, s, -jnp.inf)
        m_new = jnp.maximum(m, jnp.max(s, axis=1, keepdims=True))
        alpha = jnp.exp(m - m_new)
        p = jnp.exp(s - m_new)
        l = alpha * l + jnp.sum(p, axis=1, keepdims=True)
        acc = alpha * acc + jnp.dot(p.astype(BF16), vb, preferred_element_type=F32)
        return m_new, l, acc

    init = (jnp.full((tile, 1), -jnp.inf, F32), jnp.zeros((tile, 1), F32), jnp.zeros((tile, dh), F32))
    carry = lax.fori_loop(0, qi, functools.partial(step, diagonal=False), init)
    m, l, acc = step(qi, carry, True)
    o_ref[...] = (acc / l).astype(o_ref.dtype)


def _fox_flash(q, k, v, c):
    b, s, d = q.shape
    h = c.shape[2]
    dh = d // h
    tile = min(FOX_ATTN_TILE, s)
    ct = jnp.swapaxes(c, 1, 2)
    return pl.pallas_call(
        functools.partial(_fox_flash_kernel, tile=tile),
        grid=(b, h, s // tile),
        in_specs=[
            pl.BlockSpec((None, tile, dh), lambda bi, hi, qi: (bi, qi, hi)),
            pl.BlockSpec((None, s, dh), lambda bi, hi, qi: (bi, 0, hi)),
            pl.BlockSpec((None, s, dh), lambda bi, hi, qi: (bi, 0, hi)),
            pl.BlockSpec((None, None, tile, 1), lambda bi, hi, qi: (bi, hi, qi, 0)),
            pl.BlockSpec((None, None, 1, s), lambda bi, hi, qi: (bi, hi, 0, 0)),
        ],
        out_specs=pl.BlockSpec((None, tile, dh), lambda bi, hi, qi: (bi, qi, hi)),
        out_shape=jax.ShapeDtypeStruct((b, s, d), BF16),
        compiler_params=pltpu.CompilerParams(
            dimension_semantics=("parallel", "parallel", "arbitrary"), vmem_limit_bytes=VMEM_LIMIT_BYTES),
        name="fox_flash",
    )(q, k, v, ct[..., None], ct[:, :, None, :])


def _fox_decode_kernel(pt_ref, q_ref, kn_ref, vn_ref, c0_ref, *refs, pages, heads):
    k_refs = refs[:pages]
    v_refs = refs[pages:2 * pages]
    lf_refs = refs[2 * pages:3 * pages]
    o_ref = refs[3 * pages]
    m_ref, l_ref, acc_ref, carry_ref = refs[3 * pages + 1:]
    p = pl.program_id(1)
    rows = k_refs[0].shape[0]
    width = lf_refs[0].shape[1]

    @pl.when(p == 0)
    def _():
        m_ref[...] = jnp.full_like(m_ref, -jnp.inf)
        l_ref[...] = jnp.zeros_like(l_ref)
        acc_ref[...] = jnp.zeros_like(acc_ref)
        carry_ref[...] = c0_ref[...]

    q = q_ref[...]
    head_of_row = lax.broadcasted_iota(jnp.int32, (heads, rows), 0)
    head_of_col = lax.broadcasted_iota(jnp.int32, (heads, rows), 1) % heads
    own = head_of_row == head_of_col
    lane = lax.broadcasted_iota(jnp.int32, (SUBLANES, width), 1)
    r0 = lax.broadcasted_iota(jnp.int32, (SUBLANES, SUBLANES), 0)
    r1 = lax.broadcasted_iota(jnp.int32, (SUBLANES, SUBLANES), 1)
    later_rows = (r1 > r0).astype(F32)
    all_rows = jnp.ones((SUBLANES, SUBLANES), F32)
    carry = carry_ref[...]
    scores = []
    for g in reversed(range(pages)):
        lf = lf_refs[g][...]
        scan = lf
        total = lf
        d = heads
        while d < width:
            scan = scan + jnp.where(lane + d < width, pltpu.roll(scan, width - d, axis=1), 0.0)
            total = total + pltpu.roll(total, width - d, axis=1)
            d *= 2
        bias = scan - lf + _dot_f32(later_rows, total) + carry
        carry = carry + _dot_f32(all_rows, total)
        bias = jnp.concatenate(
            [jnp.broadcast_to(bias[r:r + 1, :], (heads, width)) for r in range(SUBLANES)], axis=1)
        s = lax.dot_general(q, k_refs[g][...], (((1,), (1,)), ((), ())), preferred_element_type=F32)
        scores.append((g, jnp.where(own, s + bias, -jnp.inf)))
    carry_ref[...] = carry
    m = m_ref[...]
    m_new = m
    for _, s in scores:
        m_new = jnp.maximum(m_new, jnp.max(s, axis=1, keepdims=True))
    alpha = jnp.exp(m - m_new)
    l = alpha * l_ref[...]
    acc = alpha * acc_ref[...]
    for g, s in scores:
        pr = jnp.exp(s - m_new)
        l = l + jnp.sum(pr, axis=1, keepdims=True)
        acc = acc + jnp.dot(pr, v_refs[g][...], preferred_element_type=F32)
    m_ref[...] = m_new
    l_ref[...] = l
    acc_ref[...] = acc

    @pl.when(p == pl.num_programs(1) - 1)
    def _():
        s_new = jnp.sum(_bf(q) * _bf(kn_ref[...]), axis=1, keepdims=True)
        m = m_ref[...]
        m_new = jnp.maximum(m, s_new)
        alpha = jnp.exp(m - m_new)
        p_new = jnp.exp(s_new - m_new)
        l = alpha * l_ref[...] + p_new
        acc = alpha * acc_ref[...] + _bf(p_new) * _bf(vn_ref[...])
        o_ref[...] = acc / l


def _fox_decode(q, k_new, v_new, log_f_new, cache_k, cache_v, cache_logf, page_table):
    b, h, dh = q.shape
    n_pool, ps = cache_k.shape[:2]
    n_pages = page_table.shape[1]
    pages = FOX_PAGES_PER_STEP
    n_steps = n_pages // pages
    rows = ps * h
    width = rows // SUBLANES
    ck = cache_k.reshape(n_pool, rows, dh)
    cv = cache_v.reshape(n_pool, rows, dh)
    clf = cache_logf.reshape(n_pool, SUBLANES, width)
    c0 = jnp.broadcast_to(jnp.tile(log_f_new, (1, width // h))[:, None, :], (b, SUBLANES, width))

    def page_map(g):
        return lambda bi, p, pt: (pt[bi, (n_steps - 1 - p) * pages + g], 0, 0)

    row_spec = pl.BlockSpec((None, h, dh), lambda bi, p, pt: (bi, 0, 0))
    grid_spec = pltpu.PrefetchScalarGridSpec(
        num_scalar_prefetch=1,
        grid=(b, n_steps),
        in_specs=[row_spec, row_spec, row_spec,
                  pl.BlockSpec((None, SUBLANES, width), lambda bi, p, pt: (bi, 0, 0))]
        + [pl.BlockSpec((None, rows, dh), page_map(g)) for g in range(pages)]
        + [pl.BlockSpec((None, rows, dh), page_map(g)) for g in range(pages)]
        + [pl.BlockSpec((None, SUBLANES, width), page_map(g)) for g in range(pages)],
        out_specs=row_spec,
        scratch_shapes=[pltpu.VMEM((h, 1), F32), pltpu.VMEM((h, 1), F32), pltpu.VMEM((h, dh), F32),
                        pltpu.VMEM((SUBLANES, width), F32)],
    )
    return pl.pallas_call(
        functools.partial(_fox_decode_kernel, pages=pages, heads=h),
        grid_spec=grid_spec,
        out_shape=jax.ShapeDtypeStruct((b, h, dh), F32),
        compiler_params=pltpu.CompilerParams(
            dimension_semantics=("parallel", "arbitrary"), vmem_limit_bytes=VMEM_LIMIT_BYTES),
        name="fox_decode",
    )(page_table, q, k_new, v_new, c0, *([ck] * pages), *([cv] * pages), *([clf] * pages))


def _gdn_scan_kernel(q_ref, k_ref, v_ref, bcol_ref, gcol_ref, grow_ref, s0_ref, o_ref, s_ref, *, rep):
    c = q_ref.shape[0]
    dk = s_ref.shape[1]
    dv = s_ref.shape[2]
    hg = s_ref.shape[0]

    @pl.when(pl.program_id(2) == 0)
    def _():
        s_ref[...] = s0_ref[...]

    row = lax.broadcasted_iota(jnp.int32, (c, c), 0)
    col = lax.broadcasted_iota(jnp.int32, (c, c), 1)
    tril = row >= col
    strict = row > col
    g_cols = _dot_f32(tril.astype(F32), gcol_ref[...])
    g_rows = _dot_f32(grow_ref[...], (row <= col).astype(F32))
    heads = []
    for h in range(hg):
        j = h // rep
        qh = q_ref[:, j * dk:(j + 1) * dk]
        kh = k_ref[:, j * dk:(j + 1) * dk]
        gc = g_cols[:, h:h + 1]
        gr = g_rows[h:h + 1, :]
        decay = jnp.where(tril, jnp.exp(jnp.where(tril, gc - gr, 0.0)), 0.0)
        beta = bcol_ref[:, h:h + 1]
        kb = kh * beta
        exp_g = jnp.exp(gc)
        low = jnp.where(strict, _dot_nt(kb, kh) * decay, 0.0)
        attn = jnp.where(tril, _dot_nt(qh, kh) * decay, 0.0)
        x = jnp.concatenate([v_ref[:, h * dv:(h + 1) * dv] * beta, kb * exp_g], axis=1)
        heads.append(dict(qh=qh, kh=kh, gc=gc, exp_g=exp_g, low=low, attn=attn, x=x))
    for hd in heads:
        hd["x"] = hd["x"] - _dot(hd["low"], hd["x"])
        hd["p"] = hd["low"]
    n = 2
    while n < c:
        for hd in heads:
            hd["p"] = _dot(hd["p"], hd["p"])
        for hd in heads:
            hd["x"] = hd["x"] + _dot(hd["p"], hd["x"])
        n *= 2
    for h, hd in enumerate(heads):
        hd["s"] = s_ref[h]
        hd["v_new"] = hd["x"][:, :dv] - _dot(hd["x"][:, dv:], hd["s"])
    for h, hd in enumerate(heads):
        o_ref[:, h * dv:(h + 1) * dv] = _dot(hd["qh"] * hd["exp_g"], hd["s"]) + _dot(hd["attn"], hd["v_new"])
    for h, hd in enumerate(heads):
        gc = hd["gc"]
        g_last = gc[c - 1:c, :]
        s_ref[h] = hd["s"] * jnp.exp(g_last) + _dot_tn(hd["kh"] * jnp.exp(g_last - gc), hd["v_new"])


def _gdn_scan(q, k, v, beta, g, s0):
    b, t, _ = q.shape
    hv, dk, dv = s0.shape[1:]
    hq = q.shape[2] // dk
    rep = hv // hq
    c = GDN_CHUNK
    nc = t // c
    hg = min(GDN_HEAD_GROUP, hv)
    ng = hv // hg
    cols = lambda a: a.reshape(b, t, ng, hg).transpose(0, 2, 1, 3)
    g_rows = g.reshape(b, nc, c, hv).transpose(0, 1, 3, 2)
    o, s = pl.pallas_call(
        functools.partial(_gdn_scan_kernel, rep=rep),
        grid=(b, ng, nc),
        in_specs=[
            pl.BlockSpec((None, c, hg // rep * dk), lambda bi, gi, ci: (bi, ci, gi)),
            pl.BlockSpec((None, c, hg // rep * dk), lambda bi, gi, ci: (bi, ci, gi)),
            pl.BlockSpec((None, c, hg * dv), lambda bi, gi, ci: (bi, ci, gi)),
            pl.BlockSpec((None, None, c, hg), lambda bi, gi, ci: (bi, gi, ci, 0)),
            pl.BlockSpec((None, None, c, hg), lambda bi, gi, ci: (bi, gi, ci, 0)),
            pl.BlockSpec((None, None, hg, c), lambda bi, gi, ci: (bi, ci, gi, 0)),
            pl.BlockSpec((None, hg, dk, dv), lambda bi, gi, ci: (bi, gi, 0, 0)),
        ],
        out_specs=[
            pl.BlockSpec((None, c, hg * dv), lambda bi, gi, ci: (bi, ci, gi)),
            pl.BlockSpec((None, hg, dk, dv), lambda bi, gi, ci: (bi, gi, 0, 0)),
        ],
        out_shape=[jax.ShapeDtypeStruct((b, t, hv * dv), F32), jax.ShapeDtypeStruct((b, hv, dk, dv), F32)],
        compiler_params=pltpu.CompilerParams(
            dimension_semantics=("parallel", "parallel", "arbitrary"), vmem_limit_bytes=VMEM_LIMIT_BYTES),
        name="gdn_scan",
    )(q, k, v, cols(beta), cols(g), g_rows, s0)
    return o, s


def _ret_scan_kernel(lg_ref, q_ref, k_ref, v_ref, s0_ref, o_ref, s_ref):
    c = q_ref.shape[0]

    @pl.when(pl.program_id(2) == 0)
    def _():
        s_ref[...] = s0_ref[...]

    lg = lg_ref[...]
    row = lax.broadcasted_iota(jnp.int32, (c, c), 0)
    col = lax.broadcasted_iota(jnp.int32, (c, c), 1)
    tril = row >= col
    decay = jnp.where(tril, jnp.exp(lg * jnp.where(tril, row - col, 0).astype(F32)), 0.0)
    pos = lax.broadcasted_iota(jnp.int32, (c, 1), 0).astype(F32)
    q = q_ref[...]
    k = k_ref[...]
    v = v_ref[...]
    s = s_ref[...]
    qk = lax.dot_general(q, k, (((1,), (1,)), ((), ())), preferred_element_type=F32)
    inner = jnp.dot(qk * decay, v, preferred_element_type=F32)
    cross = jnp.dot(q * jnp.exp(lg * (pos + 1.0)), s, preferred_element_type=F32)
    o_ref[...] = inner + cross
    kd = k * jnp.exp(lg * (c - 1.0 - pos))
    s_ref[...] = s * jnp.exp(lg * c) + lax.dot_general(kd, v, (((0,), (0,)), ((), ())), preferred_element_type=F32)


def _ret_scan(q, k, v, s0, log_gamma):
    b, t, _ = q.shape
    h, dk, dv = s0.shape[1:]
    c = RET_CHUNK
    return pl.pallas_call(
        _ret_scan_kernel,
        grid=(b, h, t // c),
        in_specs=[
            pl.BlockSpec((None, 1, 1), lambda bi, hi, ci: (hi, 0, 0)),
            pl.BlockSpec((None, c, dk), lambda bi, hi, ci: (bi, ci, hi)),
            pl.BlockSpec((None, c, dk), lambda bi, hi, ci: (bi, ci, hi)),
            pl.BlockSpec((None, c, dv), lambda bi, hi, ci: (bi, ci, hi)),
            pl.BlockSpec((None, None, dk, dv), lambda bi, hi, ci: (bi, hi, 0, 0)),
        ],
        out_specs=[
            pl.BlockSpec((None, c, dv), lambda bi, hi, ci: (bi, ci, hi)),
            pl.BlockSpec((None, None, dk, dv), lambda bi, hi, ci: (bi, hi, 0, 0)),
        ],
        out_shape=[jax.ShapeDtypeStruct((b, t, h * dv), F32), jax.ShapeDtypeStruct((b, h, dk, dv), F32)],
        compiler_params=pltpu.CompilerParams(
            dimension_semantics=("parallel", "parallel", "arbitrary"), vmem_limit_bytes=VMEM_LIMIT_BYTES),
        name="ret_scan",
    )(log_gamma.reshape(h, 1, 1), q, k, v, s0)


def _first_row(a):
    row = lax.broadcasted_iota(jnp.int32, (SUBLANES, a.shape[1]), 0)
    return jnp.where(row == 0, jnp.broadcast_to(a, (SUBLANES, a.shape[1])), 0.0)


def _gdn_step_kernel(q_ref, k_ref, v_ref, beta_ref, g_ref, s0_ref, o_ref, s_ref, *, rep):
    hv = s0_ref.shape[0]
    for h in range(hv):
        j = h // rep
        q = q_ref[j:j + 1, :]
        k = k_ref[j:j + 1, :]
        v = v_ref[h:h + 1, :]
        beta = beta_ref[h:h + 1, :]
        eg = jnp.exp(g_ref[h:h + 1, :])
        s = s0_ref[h]
        kb = k * beta
        lhs = jnp.concatenate([kb * eg, q * eg, jnp.zeros((SUBLANES - 2, q.shape[1]), F32)], axis=0)
        both = jnp.dot(lhs, s, preferred_element_type=F32)
        v_new = v * beta - both[0:1, :]
        attn = jnp.sum(_bf(q) * _bf(k), axis=1, keepdims=True)
        o_ref[h:h + 1, :] = both[1:2, :] + _bf(attn) * _bf(v_new)
        s_ref[h] = s * eg + lax.dot_general(_first_row(k), _first_row(v_new), (((0,), (0,)), ((), ())),
                                            preferred_element_type=F32)


def _gdn_step(q, k, v, beta, g, s0):
    b, hq, dk = q.shape
    hv, dv = v.shape[1:]
    spec = lambda *shape: pl.BlockSpec((None,) + shape, lambda bi: (bi,) + (0,) * len(shape))
    return pl.pallas_call(
        functools.partial(_gdn_step_kernel, rep=hv // hq),
        grid=(b,),
        in_specs=[spec(hq, dk), spec(hq, dk), spec(hv, dv), spec(hv, 1), spec(hv, 1), spec(hv, dk, dv)],
        out_specs=[spec(hv, dv), spec(hv, dk, dv)],
        out_shape=[jax.ShapeDtypeStruct((b, hv, dv), F32), jax.ShapeDtypeStruct((b, hv, dk, dv), F32)],
        compiler_params=pltpu.CompilerParams(dimension_semantics=("parallel",), vmem_limit_bytes=VMEM_LIMIT_BYTES),
        name="gdn_step",
    )(q, k, v, beta[..., None], g[..., None], s0)


def _ret_step_kernel(lg_ref, q_ref, k_ref, v_ref, s0_ref, o_ref, s_ref):
    gamma = jnp.exp(lg_ref[...])
    q = q_ref[...]
    k = k_ref[...]
    v = v_ref[...]
    s = s0_ref[...]
    cross = jnp.dot(_first_row(q * gamma), s, preferred_element_type=F32)[0:1, :]
    attn = jnp.sum(_bf(q) * _bf(k), axis=1, keepdims=True)
    o_ref[...] = _bf(attn) * _bf(v) + cross
    s_ref[...] = s * gamma + lax.dot_general(_first_row(k), _first_row(v), (((0,), (0,)), ((), ())),
                                             preferred_element_type=F32)


def _ret_step(q, k, v, s0, log_gamma):
    b, h, dk = q.shape
    dv = v.shape[2]
    vec = lambda n: pl.BlockSpec((None, None, 1, n), lambda bi, hi: (bi, hi, 0, 0))
    mat = pl.BlockSpec((None, None, dk, dv), lambda bi, hi: (bi, hi, 0, 0))
    o, s = pl.pallas_call(
        _ret_step_kernel,
        grid=(b, h),
        in_specs=[pl.BlockSpec((None, 1, 1), lambda bi, hi: (hi, 0, 0)), vec(dk), vec(dk), vec(dv), mat],
        out_specs=[vec(dv), mat],
        out_shape=[jax.ShapeDtypeStruct((b, h, 1, dv), F32), jax.ShapeDtypeStruct((b, h, dk, dv), F32)],
        compiler_params=pltpu.CompilerParams(
            dimension_semantics=("parallel", "parallel"), vmem_limit_bytes=VMEM_LIMIT_BYTES),
        name="ret_step",
    )(log_gamma.reshape(h, 1, 1), q[:, :, None, :], k[:, :, None, :], v[:, :, None, :], s0)
    return o[:, :, 0, :], s


def _rms_norm(x, g):
    return x * lax.rsqrt(jnp.mean(x * x, axis=-1, keepdims=True) + NORM_EPS) * g


def _l2_norm(x):
    return x * lax.rsqrt(jnp.sum(x * x, axis=-1, keepdims=True) + NORM_EPS)


def _rotary(x, pos):
    half = x.shape[-1] // 2
    inv = ROPE_BASE ** (-jnp.arange(half, dtype=F32) / half)
    ang = pos.astype(F32)[:, None] * inv[None, :]
    cos = jnp.cos(ang)[None, :, None, :]
    sin = jnp.sin(ang)[None, :, None, :]
    x1, x2 = x[..., :half], x[..., half:]
    return jnp.concatenate([x1 * cos - x2 * sin, x1 * sin + x2 * cos], axis=-1)


def _fox_mixer(x, b, t, g, w_in, b_f, q_norm, k_norm, w_o, cache):
    hd = (b, t, FOX_HEADS, FOX_HEAD_DIM)
    qkv = _proj(x, w_in[:, :3 * D_MODEL], gain=g)
    f = _proj(x, _pad_cols(w_in[:, 3 * D_MODEL:]), gain=g)[:, :FOX_HEADS]
    q = _rms_norm(qkv[:, :D_MODEL].reshape(hd), q_norm) * FOX_HEAD_DIM ** -0.5
    k = _rms_norm(qkv[:, D_MODEL:2 * D_MODEL].reshape(hd), k_norm)
    v = qkv[:, 2 * D_MODEL:].reshape(hd)
    log_f = jax.nn.log_sigmoid(f.reshape(b, t, FOX_HEADS) + b_f)
    if cache is None:
        flat = lambda a: a.reshape(b, t, D_MODEL).astype(BF16)
        o = _fox_flash(flat(q), flat(k), flat(v), jnp.cumsum(log_f, axis=1))
    else:
        assert t == 1
        o = _fox_decode(q[:, 0], k[:, 0], v[:, 0], log_f[:, 0], *cache).astype(BF16)
    y = _proj(o.reshape(b * t, D_MODEL), w_o, residual=x)
    return y, (k, v, log_f)


def _gdn_mixer(x, b, t, g, w_in, conv_w, a_log, dt_bias, o_norm, w_o, conv_buf, s0):
    n_main = GDN_CONV_DIM + GDN_V_DIM
    main = _proj(x, w_in[:, :n_main], gain=g)
    tail = _proj(x, _pad_cols(w_in[:, n_main:]), gain=g)
    qkv = main[:, :GDN_CONV_DIM].reshape(b, t, GDN_CONV_DIM)
    z = main[:, GDN_CONV_DIM:].reshape(b, t, GDN_V_HEADS, GDN_DV)
    b_in = tail[:, :GDN_V_HEADS].reshape(b, t, GDN_V_HEADS)
    a_in = tail[:, GDN_V_HEADS:2 * GDN_V_HEADS].reshape(b, t, GDN_V_HEADS)
    xc = jnp.concatenate([conv_buf, qkv], axis=1)
    conv = xc[:, 0:t] * conv_w[0]
    for w in range(1, GDN_CONV_WIDTH):
        conv = conv + xc[:, w:w + t] * conv_w[w]
    conv = jax.nn.silu(conv)
    new_buf = xc[:, t:]
    q, k, v = jnp.split(conv, [GDN_QK_DIM, 2 * GDN_QK_DIM], axis=-1)
    q = _l2_norm(q.reshape(b, t, GDN_QK_HEADS, GDN_DK)) * GDN_DK ** -0.5
    k = _l2_norm(k.reshape(b, t, GDN_QK_HEADS, GDN_DK))
    beta = jax.nn.sigmoid(b_in)
    gg = -jnp.exp(a_log) * jax.nn.softplus(a_in + dt_bias)
    if t % GDN_CHUNK == 0:
        o, s_new = _gdn_scan(q.reshape(b, t, GDN_QK_DIM), k.reshape(b, t, GDN_QK_DIM), v, beta, gg, s0)
    else:
        assert t == 1
        o, s_new = _gdn_step(q[:, 0], k[:, 0], v.reshape(b, GDN_V_HEADS, GDN_DV), beta[:, 0], gg[:, 0], s0)
    o = _rms_norm(o.reshape(b, t, GDN_V_HEADS, GDN_DV), o_norm) * jax.nn.silu(z)
    y = _proj(o.reshape(b * t, GDN_V_DIM).astype(BF16), w_o, residual=x)
    return y, (new_buf, s_new)


def _ret_mixer(x, b, t, g, w_in, gn_gain, w_o, s0, pos0):
    proj = _proj(x, w_in, gain=g)
    q = proj[:, :D_MODEL].reshape(b, t, RET_HEADS, RET_DK)
    k = proj[:, D_MODEL:2 * D_MODEL].reshape(b, t, RET_HEADS, RET_DK)
    v = proj[:, 2 * D_MODEL:2 * D_MODEL + RET_V_DIM]
    gate = proj[:, 2 * D_MODEL + RET_V_DIM:]
    pos = pos0 + jnp.arange(t)
    q = _rotary(q, pos)
    k = _rotary(k, pos) * RET_DK ** -0.5
    log_gamma = jnp.log1p(-jnp.exp2(-5.0 - jnp.arange(RET_HEADS, dtype=F32)))
    if t % RET_CHUNK == 0:
        o, s_new = _ret_scan(q.reshape(b, t, D_MODEL), k.reshape(b, t, D_MODEL), v.reshape(b, t, RET_V_DIM), s0,
                             log_gamma)
    else:
        assert t == 1
        o, s_new = _ret_step(q[:, 0], k[:, 0], v.reshape(b, RET_HEADS, RET_DV), s0, log_gamma)
    o = o.reshape(b, t, RET_HEADS, RET_DV)
    mu = jnp.mean(o, axis=-1, keepdims=True)
    var = jnp.mean(jnp.square(o - mu), axis=-1, keepdims=True)
    o = ((o - mu) * lax.rsqrt(var + GN_EPS)).reshape(b * t, RET_V_DIM) * gn_gain
    y = _proj((jax.nn.silu(gate) * o).astype(BF16), w_o, residual=x)
    return y, (s_new,)


def kernel(x_prompt, x_sample, cache_k_l0, cache_v_l0, cache_logf_l0, state_conv_l1, state_gdn_l1, state_ret_l2, cache_k_l3, cache_v_l3, cache_logf_l3, page_table, norm_g, ffn_w_in, ffn_w_out, fox_w_in_l0, fox_b_f_l0, fox_q_norm_l0, fox_k_norm_l0, fox_w_o_l0, gdn_w_in_l1, gdn_conv_w_l1, gdn_a_log_l1, gdn_dt_bias_l1, gdn_o_norm_l1, gdn_w_o_l1, ret_w_in_l2, ret_gn_l2, ret_w_o_l2, fox_w_in_l3, fox_b_f_l3, fox_q_norm_l3, fox_k_norm_l3, fox_w_o_l3):
    ffn_in = ffn_w_in.astype(BF16)
    ffn_out = ffn_w_out.astype(BF16)
    fox0 = (fox_w_in_l0.astype(BF16), fox_b_f_l0, fox_q_norm_l0, fox_k_norm_l0, fox_w_o_l0.astype(BF16))
    fox3 = (fox_w_in_l3.astype(BF16), fox_b_f_l3, fox_q_norm_l3, fox_k_norm_l3, fox_w_o_l3.astype(BF16))
    gdn = (gdn_w_in_l1.astype(BF16), gdn_conv_w_l1, gdn_a_log_l1, gdn_dt_bias_l1, gdn_o_norm_l1,
           gdn_w_o_l1.astype(BF16))
    ret = (ret_w_in_l2.astype(BF16), ret_gn_l2, ret_w_o_l2.astype(BF16))

    def run(x3, sample):
        b, t, _ = x3.shape
        x = x3.reshape(b * t, D_MODEL)
        states = []
        for i in range(DEPTH):
            x = _ffn(x, norm_g[i, 0], ffn_in, ffn_out, i, 0)
            g = norm_g[i, 1]
            if i == 0:
                cache = (cache_k_l0, cache_v_l0, cache_logf_l0, page_table) if sample else None
                x, st = _fox_mixer(x, b, t, g, *fox0, cache)
            elif i == 1:
                if sample:
                    conv0, s0 = state_conv_l1, state_gdn_l1
                else:
                    conv0 = jnp.zeros((b, GDN_CONV_WIDTH - 1, GDN_CONV_DIM), F32)
                    s0 = jnp.zeros((b, GDN_V_HEADS, GDN_DK, GDN_DV), F32)
                x, st = _gdn_mixer(x, b, t, g, *gdn, conv0, s0)
            elif i == 2:
                s0 = state_ret_l2 if sample else jnp.zeros((b, RET_HEADS, RET_DK, RET_DV), F32)
                x, st = _ret_mixer(x, b, t, g, *ret, s0, PAST_LEN if sample else 0)
            else:
                cache = (cache_k_l3, cache_v_l3, cache_logf_l3, page_table) if sample else None
                x, st = _fox_mixer(x, b, t, g, *fox3, cache)
            x = _ffn(x, norm_g[i, 2], ffn_in, ffn_out, i, 1)
            states.append(st)
        return x.reshape(b, t, D_MODEL), states

    yp, st_p = run(x_prompt, False)
    ys, st_s = run(x_sample, True)
    new_state = []
    for sp, ss in zip(st_p, st_s):
        new_state += [*sp, *ss]
    return (yp, ys, *new_state)
```

```python
import functools

import jax
import jax.numpy as jnp
from jax import lax
from jax.experimental import pallas as pl
from jax.experimental.pallas import tpu as pltpu

F32 = jnp.float32
BF16 = jnp.bfloat16

D_MODEL = 2048
DEPTH = 4
PAST_LEN = 16384
D_FF = 5632
NORM_EPS = 1e-6
GN_EPS = 1e-5
FFN_HALF = 0.5

FOX_HEADS = 16
FOX_HEAD_DIM = D_MODEL // FOX_HEADS

GDN_QK_HEADS = 16
GDN_V_HEADS = 32
GDN_DK = 128
GDN_DV = 128
GDN_QK_DIM = GDN_QK_HEADS * GDN_DK
GDN_V_DIM = GDN_V_HEADS * GDN_DV
GDN_CONV_DIM = 2 * GDN_QK_DIM + GDN_V_DIM
GDN_CONV_WIDTH = 4
GDN_CHUNK = 64

RET_HEADS = 8
RET_DK = D_MODEL // RET_HEADS
RET_DV = 2 * RET_DK
RET_V_DIM = RET_HEADS * RET_DV
RET_CHUNK = 128
ROPE_BASE = 10000.0
LOG2E = 1.4426950408889634

LANES = 128
SUBLANES = 8
VMEM_LIMIT_BYTES = 56 * 1024 * 1024
ROW_TILE = 512
PROJ_ROW_TILE = 1024
COL_TILE = 1024
FF_TILE = 512
FOX_ATTN_TILE = 512
FOX_PAGES_PER_STEP = 4
GDN_HEAD_GROUP = 8


def _row_tile(m, tile=ROW_TILE):
    return tile if m % tile == 0 else m


def _col_tile(n):
    for t in (COL_TILE, 768, 512, 256, 128):
        if n % t == 0:
            return t
    return n


def _dot(a, b):
    return jnp.dot(a.astype(BF16), b.astype(BF16), preferred_element_type=F32)


def _dot_nt(a, b):
    return lax.dot_general(a.astype(BF16), b.astype(BF16), (((1,), (1,)), ((), ())), preferred_element_type=F32)


def _dot_tn(a, b):
    return lax.dot_general(a.astype(BF16), b.astype(BF16), (((0,), (0,)), ((), ())), preferred_element_type=F32)


def _dot_f32(a, b):
    return jnp.dot(a, b, preferred_element_type=F32, precision=lax.Precision.HIGHEST)


def _bf(a):
    return a.astype(BF16).astype(F32)


def _gated_rms(o, gain, z):
    o = o * lax.rsqrt(jnp.mean(o * o, axis=-1, keepdims=True) + NORM_EPS) * gain
    return o * (z * jax.nn.sigmoid(z))


def _gated_group_norm(o, gain, gate):
    mu = jnp.mean(o, axis=-1, keepdims=True)
    var = jnp.mean(jnp.square(o - mu), axis=-1, keepdims=True)
    o = (o - mu) * lax.rsqrt(var + GN_EPS) * gain
    return (gate * jax.nn.sigmoid(gate)) * o


def _ffn_kernel(x_ref, g_ref, wg_ref, wu_ref, wo_ref, o_ref, xn_ref):
    j = pl.program_id(1)

    @pl.when(j == 0)
    def _():
        x = x_ref[...]
        inv = lax.rsqrt(jnp.mean(x * x, axis=-1, keepdims=True) + NORM_EPS)
        xn_ref[...] = (x * inv * g_ref[...]).astype(BF16)
        o_ref[...] = jnp.zeros_like(o_ref)

    xn = xn_ref[...]
    gate = jnp.dot(xn, wg_ref[...], preferred_element_type=F32)
    up = jnp.dot(xn, wu_ref[...], preferred_element_type=F32)
    act = (gate * jax.nn.sigmoid(gate) * up).astype(BF16)
    o_ref[...] += jnp.dot(act, wo_ref[...], preferred_element_type=F32)

    @pl.when(j == pl.num_programs(1) - 1)
    def _():
        o_ref[...] = x_ref[...] + FFN_HALF * o_ref[...]


def _ffn(x, g, w_in, w_out, layer, which):
    m, d = x.shape
    f = w_out.shape[2]
    tm = _row_tile(m)
    tf = FF_TILE
    nf = f // tf
    return pl.pallas_call(
        _ffn_kernel,
        grid=(m // tm, nf),
        in_specs=[
            pl.BlockSpec((tm, d), lambda i, j: (i, 0)),
            pl.BlockSpec((1, d), lambda i, j: (0, 0)),
            pl.BlockSpec((None, None, d, tf), lambda i, j: (layer, which, 0, j)),
            pl.BlockSpec((None, None, d, tf), lambda i, j: (layer, which, 0, j + nf)),
            pl.BlockSpec((None, None, tf, d), lambda i, j: (layer, which, j, 0)),
        ],
        out_specs=pl.BlockSpec((tm, d), lambda i, j: (i, 0)),
        out_shape=jax.ShapeDtypeStruct((m, d), F32),
        scratch_shapes=[pltpu.VMEM((tm, d), BF16)],
        compiler_params=pltpu.CompilerParams(
            dimension_semantics=("parallel", "arbitrary"), vmem_limit_bytes=VMEM_LIMIT_BYTES),
        name="ffn",
    )(x, g.reshape(1, d), w_in, w_in, w_out)


def _proj_kernel(*refs, norm, residual, epilogue, n_extra, n_out, head_dim, post_scale):
    refs = list(refs)
    x_ref = refs.pop(0)
    g_ref = refs.pop(0) if norm else None
    w_ref = refs.pop(0)
    r_ref = refs.pop(0) if residual else None
    extra = [refs.pop(0) for _ in range(n_extra)]
    outs = [refs.pop(0) for _ in range(n_out)]
    if norm:
        xn_ref = refs.pop(0)

        @pl.when(pl.program_id(1) == 0)
        def _():
            x = x_ref[...]
            inv = lax.rsqrt(jnp.mean(x * x, axis=-1, keepdims=True) + NORM_EPS)
            xn_ref[...] = (x * inv * g_ref[...]).astype(BF16)

        lhs = xn_ref[...]
    else:
        lhs = x_ref[...]
    y = jnp.dot(lhs, w_ref[...], preferred_element_type=F32)
    if residual:
        y = r_ref[...] + y

    def emit(lo, hi, val):
        for o_ref in outs:
            o_ref[:, lo:hi] = val.astype(o_ref.dtype)

    tn = y.shape[1]
    if epilogue is None:
        emit(0, tn, y)
    elif epilogue == "head_rms":
        gain = extra[0][...]
        for lo in range(0, tn, head_dim):
            yh = y[:, lo:lo + head_dim]
            yh = yh * lax.rsqrt(jnp.mean(yh * yh, axis=-1, keepdims=True) + NORM_EPS) * gain
            emit(lo, lo + head_dim, yh if post_scale is None else yh * post_scale)
    elif epilogue == "rotary":
        cos = extra[0][...]
        sin = extra[1][...]
        half = head_dim // 2
        for lo in range(0, tn, head_dim):
            x1 = y[:, lo:lo + half]
            x2 = y[:, lo + half:lo + head_dim]
            o1 = x1 * cos - x2 * sin
            o2 = x1 * sin + x2 * cos
            emit(lo, lo + half, o1 if post_scale is None else o1 * post_scale)
            emit(lo + half, lo + head_dim, o2 if post_scale is None else o2 * post_scale)
    elif epilogue == "log_sigmoid":
        emit(0, tn, jax.nn.log_sigmoid(y + extra[0][...]))
    else:
        raise ValueError(epilogue)


def _proj(x, w, gain=None, residual=None, epilogue=None, extra=(), out_dtypes=(F32,), head_dim=None, post_scale=None):
    m, k = x.shape
    n = w.shape[1]
    tm = _row_tile(m, PROJ_ROW_TILE)
    tn = _col_tile(n)
    norm = gain is not None
    in_specs = [pl.BlockSpec((tm, k), lambda i, j: (i, 0))]
    args = [x]
    if norm:
        in_specs.append(pl.BlockSpec((1, k), lambda i, j: (0, 0)))
        args.append(gain.reshape(1, k))
    in_specs.append(pl.BlockSpec((k, tn), lambda i, j: (0, j)))
    args.append(w)
    if residual is not None:
        in_specs.append(pl.BlockSpec((tm, tn), lambda i, j: (i, j)))
        args.append(residual)
    if epilogue == "head_rms":
        in_specs.append(pl.BlockSpec((1, head_dim), lambda i, j: (0, 0)))
        args.append(extra[0].reshape(1, head_dim))
    elif epilogue == "rotary":
        for e in extra:
            in_specs.append(pl.BlockSpec((tm, head_dim // 2), lambda i, j: (i, 0)))
            args.append(e)
    elif epilogue == "log_sigmoid":
        in_specs.append(pl.BlockSpec((1, tn), lambda i, j: (0, j)))
        args.append(extra[0].reshape(1, n))
    outs = pl.pallas_call(
        functools.partial(_proj_kernel, norm=norm, residual=residual is not None, epilogue=epilogue,
                          n_extra=len(extra), n_out=len(out_dtypes), head_dim=head_dim, post_scale=post_scale),
        grid=(m // tm, n // tn),
        in_specs=in_specs,
        out_specs=[pl.BlockSpec((tm, tn), lambda i, j: (i, j)) for _ in out_dtypes],
        out_shape=[jax.ShapeDtypeStruct((m, n), dt) for dt in out_dtypes],
        scratch_shapes=[pltpu.VMEM((tm, k), BF16)] if norm else [],
        compiler_params=pltpu.CompilerParams(
            dimension_semantics=("parallel", "arbitrary"), vmem_limit_bytes=VMEM_LIMIT_BYTES),
        name="proj" if epilogue is None else "proj_" + epilogue,
    )(*args)
    return outs[0] if len(outs) == 1 else outs


def _pad_cols(w):
    return jnp.pad(w, ((0, 0),) * (w.ndim - 1) + ((0, LANES - w.shape[-1]),))


def _fox_flash_kernel(q_ref, k_ref, v_ref, ck_ref, o_ref, *, tile):
    qi = pl.program_id(2)
    q = q_ref[...]
    dh = q.shape[1]

    def step(j, carry, diagonal):
        m, l, acc = carry
        ks = pl.multiple_of(j * tile, tile)
        kb = k_ref[pl.ds(ks, tile), :]
        vb = v_ref[pl.ds(ks, tile), :]
        ck = ck_ref[:, pl.ds(ks, tile)] * LOG2E
        s = lax.dot_general(q, kb, (((1,), (1,)), ((), ())), preferred_element_type=F32) - ck
        if diagonal:
            row = lax.broadcasted_iota(jnp.int32, s.shape, 0)
            col = lax.broadcasted_iota(jnp.int32, s.shape, 1)
            s = jnp.where(col <= row, s, -jnp.inf)
        m_new = jnp.maximum(m, jnp.max(s, axis=1, keepdims=True))
        alpha = jnp.exp2(m - m_new)
        p = jnp.exp2(s - m_new)
        l = alpha * l + jnp.sum(p, axis=1, keepdims=True)
        acc = alpha * acc + jnp.dot(p.astype(BF16), vb, preferred_element_type=F32)
        return m_new, l, acc

    init = (jnp.full((tile, 1), -jnp.inf, F32), jnp.zeros((tile, 1), F32), jnp.zeros((tile, dh), F32))
    carry = lax.fori_loop(0, qi, functools.partial(step, diagonal=False), init)
    m, l, acc = step(qi, carry, True)
    o_ref[...] = (acc / l).astype(o_ref.dtype)


def _fox_flash(q, k, v, c):
    b, s, d = q.shape
    h = c.shape[2]
    dh = d // h
    tile = min(FOX_ATTN_TILE, s)
    ct = jnp.swapaxes(c, 1, 2)
    return pl.pallas_call(
        functools.partial(_fox_flash_kernel, tile=tile),
        grid=(b, h, s // tile),
        in_specs=[
            pl.BlockSpec((None, tile, dh), lambda bi, hi, qi: (bi, qi, hi)),
            pl.BlockSpec((None, s, dh), lambda bi, hi, qi: (bi, 0, hi)),
            pl.BlockSpec((None, s, dh), lambda bi, hi, qi: (bi, 0, hi)),
            pl.BlockSpec((None, None, 1, s), lambda bi, hi, qi: (bi, hi, 0, 0)),
        ],
        out_specs=pl.BlockSpec((None, tile, dh), lambda bi, hi, qi: (bi, qi, hi)),
        out_shape=jax.ShapeDtypeStruct((b, s, d), BF16),
        compiler_params=pltpu.CompilerParams(
            dimension_semantics=("parallel", "parallel", "arbitrary"), vmem_limit_bytes=VMEM_LIMIT_BYTES),
        name="fox_flash",
    )(q, k, v, ct[:, :, None, :])


def _fox_decode_kernel(pt_ref, q_ref, kn_ref, vn_ref, c0_ref, *refs, pages, heads):
    k_refs = refs[:pages]
    v_refs = refs[pages:2 * pages]
    lf_refs = refs[2 * pages:3 * pages]
    o_ref = refs[3 * pages]
    m_ref, l_ref, acc_ref, carry_ref = refs[3 * pages + 1:]
    p = pl.program_id(1)
    rows = k_refs[0].shape[0]
    width = lf_refs[0].shape[1]

    @pl.when(p == 0)
    def _():
        m_ref[...] = jnp.full_like(m_ref, -jnp.inf)
        l_ref[...] = jnp.zeros_like(l_ref)
        acc_ref[...] = jnp.zeros_like(acc_ref)
        carry_ref[...] = c0_ref[...]

    q = q_ref[...]
    head_of_row = lax.broadcasted_iota(jnp.int32, (heads, rows), 0)
    head_of_col = lax.broadcasted_iota(jnp.int32, (heads, rows), 1) % heads
    own = head_of_row == head_of_col
    lane = lax.broadcasted_iota(jnp.int32, (SUBLANES, width), 1)
    r0 = lax.broadcasted_iota(jnp.int32, (SUBLANES, SUBLANES), 0)
    r1 = lax.broadcasted_iota(jnp.int32, (SUBLANES, SUBLANES), 1)
    later_rows = (r1 > r0).astype(F32)
    all_rows = jnp.ones((SUBLANES, SUBLANES), F32)
    carry = carry_ref[...]
    scores = []
    for g in reversed(range(pages)):
        lf = lf_refs[g][...]
        scan = lf
        total = lf
        d = heads
        while d < width:
            scan = scan + jnp.where(lane + d < width, pltpu.roll(scan, width - d, axis=1), 0.0)
            total = total + pltpu.roll(total, width - d, axis=1)
            d *= 2
        bias = scan - lf + _dot_f32(later_rows, total) + carry
        carry = carry + _dot_f32(all_rows, total)
        bias = jnp.concatenate(
            [jnp.broadcast_to(bias[r:r + 1, :], (heads, width)) for r in range(SUBLANES)], axis=1)
        s = lax.dot_general(q, k_refs[g][...], (((1,), (1,)), ((), ())), preferred_element_type=F32)
        scores.append((g, jnp.where(own, s + bias, -jnp.inf)))
    carry_ref[...] = carry
    m = m_ref[...]
    m_new = m
    for _, s in scores:
        m_new = jnp.maximum(m_new, jnp.max(s, axis=1, keepdims=True))
    alpha = jnp.exp(m - m_new)
    l = alpha * l_ref[...]
    acc = alpha * acc_ref[...]
    for g, s in scores:
        pr = jnp.exp(s - m_new)
        l = l + jnp.sum(pr, axis=1, keepdims=True)
        acc = acc + jnp.dot(pr, v_refs[g][...], preferred_element_type=F32)
    m_ref[...] = m_new
    l_ref[...] = l
    acc_ref[...] = acc

    @pl.when(p == pl.num_programs(1) - 1)
    def _():
        s_new = jnp.sum(_bf(q) * _bf(kn_ref[...]), axis=1, keepdims=True)
        m = m_ref[...]
        m_new = jnp.maximum(m, s_new)
        alpha = jnp.exp(m - m_new)
        p_new = jnp.exp(s_new - m_new)
        l = alpha * l_ref[...] + p_new
        acc = alpha * acc_ref[...] + _bf(p_new) * _bf(vn_ref[...])
        o_ref[...] = acc / l


def _fox_decode(q, k_new, v_new, log_f_new, cache_k, cache_v, cache_logf, page_table):
    b, h, dh = q.shape
    n_pool, ps = cache_k.shape[:2]
    n_pages = page_table.shape[1]
    pages = FOX_PAGES_PER_STEP
    n_steps = n_pages // pages
    rows = ps * h
    width = rows // SUBLANES
    ck = cache_k.reshape(n_pool, rows, dh)
    cv = cache_v.reshape(n_pool, rows, dh)
    clf = cache_logf.reshape(n_pool, SUBLANES, width)
    c0 = jnp.broadcast_to(jnp.tile(log_f_new, (1, width // h))[:, None, :], (b, SUBLANES, width))

    def page_map(g):
        return lambda bi, p, pt: (pt[bi, (n_steps - 1 - p) * pages + g], 0, 0)

    row_spec = pl.BlockSpec((None, h, dh), lambda bi, p, pt: (bi, 0, 0))
    grid_spec = pltpu.PrefetchScalarGridSpec(
        num_scalar_prefetch=1,
        grid=(b, n_steps),
        in_specs=[row_spec, row_spec, row_spec,
                  pl.BlockSpec((None, SUBLANES, width), lambda bi, p, pt: (bi, 0, 0))]
        + [pl.BlockSpec((None, rows, dh), page_map(g)) for g in range(pages)]
        + [pl.BlockSpec((None, rows, dh), page_map(g)) for g in range(pages)]
        + [pl.BlockSpec((None, SUBLANES, width), page_map(g)) for g in range(pages)],
        out_specs=row_spec,
        scratch_shapes=[pltpu.VMEM((h, 1), F32), pltpu.VMEM((h, 1), F32), pltpu.VMEM((h, dh), F32),
                        pltpu.VMEM((SUBLANES, width), F32)],
    )
    return pl.pallas_call(
        functools.partial(_fox_decode_kernel, pages=pages, heads=h),
        grid_spec=grid_spec,
        out_shape=jax.ShapeDtypeStruct((b, h, dh), F32),
        compiler_params=pltpu.CompilerParams(
            dimension_semantics=("parallel", "arbitrary"), vmem_limit_bytes=VMEM_LIMIT_BYTES),
        name="fox_decode",
    )(page_table, q, k_new, v_new, c0, *([ck] * pages), *([cv] * pages), *([clf] * pages))


def _conv_silu_kernel(cur_ref, prev_ref, st_ref, w_ref, o_ref, *, tiles_per_seq, head_dim, post_scale):
    taps = w_ref.shape[0]
    cur = cur_ref[...]
    first = pl.program_id(0) % tiles_per_seq == 0
    prev = jnp.where(first, st_ref[...], prev_ref[...])
    w = w_ref[...]

    def finish(conv):
        y = conv * jax.nn.sigmoid(conv)
        if head_dim is None:
            return y
        parts = []
        for lo in range(0, y.shape[1], head_dim):
            yh = y[:, lo:lo + head_dim]
            yh = yh * lax.rsqrt(jnp.sum(yh * yh, axis=-1, keepdims=True) + NORM_EPS)
            parts.append(yh if post_scale is None else yh * post_scale)
        return jnp.concatenate(parts, axis=1)

    conv = cur * w[taps - 1:taps, :]
    for s in range(1, taps):
        conv = conv + pltpu.roll(cur, s, axis=0) * w[taps - 1 - s:taps - s, :]
    o_ref[...] = finish(conv)
    top = cur[0:SUBLANES, :]
    row = lax.broadcasted_iota(jnp.int32, top.shape, 0)
    conv = top * w[taps - 1:taps, :]
    for s in range(1, taps):
        shifted = jnp.where(row < s, pltpu.roll(prev, s, axis=0), pltpu.roll(top, s, axis=0))
        conv = conv + shifted * w[taps - 1 - s:taps - s, :]
    o_ref[0:SUBLANES, :] = finish(conv)


def _conv_silu(x, state, w, col0, ncols, seq_len, head_dim=None, post_scale=None):
    m, c = x.shape
    b = m // seq_len
    taps = w.shape[0]
    tm = min(ROW_TILE, seq_len)
    tn = min(COL_TILE, ncols)
    tiles_per_seq = seq_len // tm
    j0 = col0 // tn
    st = jnp.pad(state, ((0, 0), (SUBLANES - (taps - 1), 0), (0, 0)))
    return pl.pallas_call(
        functools.partial(_conv_silu_kernel, tiles_per_seq=tiles_per_seq, head_dim=head_dim, post_scale=post_scale),
        grid=(m // tm, ncols // tn),
        in_specs=[
            pl.BlockSpec((tm, tn), lambda i, j: (i, j + j0)),
            pl.BlockSpec((SUBLANES, tn), lambda i, j: (jnp.maximum(i * (tm // SUBLANES) - 1, 0), j + j0)),
            pl.BlockSpec((None, SUBLANES, tn), lambda i, j: (i // tiles_per_seq, 0, j + j0)),
            pl.BlockSpec((taps, tn), lambda i, j: (0, j + j0)),
        ],
        out_specs=pl.BlockSpec((tm, tn), lambda i, j: (i, j)),
        out_shape=jax.ShapeDtypeStruct((m, ncols), F32),
        compiler_params=pltpu.CompilerParams(
            dimension_semantics=("parallel", "parallel"), vmem_limit_bytes=VMEM_LIMIT_BYTES),
        name="conv_silu",
    )(x, x, st, w)


def _gdn_scan_kernel(q_ref, k_ref, v_ref, bcol_ref, gcol_ref, grow_ref, z_ref, gn_ref, s0_ref, o_ref, s_ref, *,
                     rep):
    c = q_ref.shape[0]
    dk = s_ref.shape[1]
    dv = s_ref.shape[2]
    hg = s_ref.shape[0]

    @pl.when(pl.program_id(2) == 0)
    def _():
        s_ref[...] = s0_ref[...]

    row = lax.broadcasted_iota(jnp.int32, (c, c), 0)
    col = lax.broadcasted_iota(jnp.int32, (c, c), 1)
    tril = row >= col
    strict = row > col
    g_cols = _dot_f32(tril.astype(F32), gcol_ref[...])
    g_rows = _dot_f32(grow_ref[...], (row <= col).astype(F32))
    heads = []
    for h in range(hg):
        j = h // rep
        qh = q_ref[:, j * dk:(j + 1) * dk]
        kh = k_ref[:, j * dk:(j + 1) * dk]
        gc = g_cols[:, h:h + 1]
        gr = g_rows[h:h + 1, :]
        decay = jnp.where(tril, jnp.exp(jnp.where(tril, gc - gr, 0.0)), 0.0)
        beta = bcol_ref[:, h:h + 1]
        kb = kh * beta
        exp_g = jnp.exp(gc)
        low = jnp.where(strict, _dot_nt(kb, kh) * decay, 0.0)
        attn = jnp.where(tril, _dot_nt(qh, kh) * decay, 0.0)
        x = jnp.concatenate([v_ref[:, h * dv:(h + 1) * dv] * beta, kb * exp_g], axis=1)
        heads.append(dict(qh=qh, kh=kh, gc=gc, exp_g=exp_g, low=low, attn=attn, x=x))
    for hd in heads:
        hd["x"] = hd["x"] - _dot(hd["low"], hd["x"])
        hd["p"] = hd["low"]
    n = 2
    while n < c:
        for hd in heads:
            hd["p"] = _dot(hd["p"], hd["p"])
        for hd in heads:
            hd["x"] = hd["x"] + _dot(hd["p"], hd["x"])
        n *= 2
    for h, hd in enumerate(heads):
        hd["s"] = s_ref[h]
        hd["v_new"] = hd["x"][:, :dv] - _dot(hd["x"][:, dv:], hd["s"])
    for h, hd in enumerate(heads):
        o = _dot(hd["qh"] * hd["exp_g"], hd["s"]) + _dot(hd["attn"], hd["v_new"])
        o_ref[:, h * dv:(h + 1) * dv] = _gated_rms(o, gn_ref[...], z_ref[:, h * dv:(h + 1) * dv]).astype(o_ref.dtype)
    for h, hd in enumerate(heads):
        gc = hd["gc"]
        g_last = gc[c - 1:c, :]
        s_ref[h] = hd["s"] * jnp.exp(g_last) + _dot_tn(hd["kh"] * jnp.exp(g_last - gc), hd["v_new"])


def _gdn_scan(q, k, v, beta, g, z, o_norm, s0):
    b, t, _ = q.shape
    hv, dk, dv = s0.shape[1:]
    hq = q.shape[2] // dk
    rep = hv // hq
    c = GDN_CHUNK
    nc = t // c
    hg = min(GDN_HEAD_GROUP, hv)
    ng = hv // hg
    cols = lambda a: a.reshape(b, t, ng, hg).transpose(0, 2, 1, 3)
    g_rows = g.reshape(b, nc, c, hv).transpose(0, 1, 3, 2)
    o, s = pl.pallas_call(
        functools.partial(_gdn_scan_kernel, rep=rep),
        grid=(b, ng, nc),
        in_specs=[
            pl.BlockSpec((None, c, hg // rep * dk), lambda bi, gi, ci: (bi, ci, gi)),
            pl.BlockSpec((None, c, hg // rep * dk), lambda bi, gi, ci: (bi, ci, gi)),
            pl.BlockSpec((None, c, hg * dv), lambda bi, gi, ci: (bi, ci, gi)),
            pl.BlockSpec((None, None, c, hg), lambda bi, gi, ci: (bi, gi, ci, 0)),
            pl.BlockSpec((None, None, c, hg), lambda bi, gi, ci: (bi, gi, ci, 0)),
            pl.BlockSpec((None, None, hg, c), lambda bi, gi, ci: (bi, ci, gi, 0)),
            pl.BlockSpec((None, c, hg * dv), lambda bi, gi, ci: (bi, ci, gi)),
            pl.BlockSpec((1, dv), lambda bi, gi, ci: (0, 0)),
            pl.BlockSpec((None, hg, dk, dv), lambda bi, gi, ci: (bi, gi, 0, 0)),
        ],
        out_specs=[
            pl.BlockSpec((None, c, hg * dv), lambda bi, gi, ci: (bi, ci, gi)),
            pl.BlockSpec((None, hg, dk, dv), lambda bi, gi, ci: (bi, gi, 0, 0)),
        ],
        out_shape=[jax.ShapeDtypeStruct((b, t, hv * dv), BF16), jax.ShapeDtypeStruct((b, hv, dk, dv), F32)],
        compiler_params=pltpu.CompilerParams(
            dimension_semantics=("parallel", "parallel", "arbitrary"), vmem_limit_bytes=VMEM_LIMIT_BYTES),
        name="gdn_scan",
    )(q, k, v, cols(beta), cols(g), g_rows, z, o_norm.reshape(1, dv), s0)
    return o, s


def _ret_scan_kernel(lg_ref, q_ref, k_ref, v_ref, gate_ref, gn_ref, s0_ref, o_ref, s_ref):
    c = q_ref.shape[0]

    @pl.when(pl.program_id(2) == 0)
    def _():
        s_ref[...] = s0_ref[...]

    lg = lg_ref[...]
    row = lax.broadcasted_iota(jnp.int32, (c, c), 0)
    col = lax.broadcasted_iota(jnp.int32, (c, c), 1)
    tril = row >= col
    decay = jnp.where(tril, jnp.exp(lg * jnp.where(tril, row - col, 0).astype(F32)), 0.0)
    pos = lax.broadcasted_iota(jnp.int32, (c, 1), 0).astype(F32)
    q = q_ref[...]
    k = k_ref[...]
    v = v_ref[...]
    s = s_ref[...]
    qk = lax.dot_general(q, k, (((1,), (1,)), ((), ())), preferred_element_type=F32)
    inner = jnp.dot(qk * decay, v, preferred_element_type=F32)
    cross = jnp.dot(q * jnp.exp(lg * (pos + 1.0)), s, preferred_element_type=F32)
    o_ref[...] = _gated_group_norm(inner + cross, gn_ref[...], gate_ref[...]).astype(o_ref.dtype)
    kd = k * jnp.exp(lg * (c - 1.0 - pos))
    s_ref[...] = s * jnp.exp(lg * c) + lax.dot_general(kd, v, (((0,), (0,)), ((), ())), preferred_element_type=F32)


def _ret_scan(q, k, v, gate, gn_gain, s0, log_gamma):
    b, t, _ = q.shape
    h, dk, dv = s0.shape[1:]
    c = RET_CHUNK
    return pl.pallas_call(
        _ret_scan_kernel,
        grid=(b, h, t // c),
        in_specs=[
            pl.BlockSpec((None, 1, 1), lambda bi, hi, ci: (hi, 0, 0)),
            pl.BlockSpec((None, c, dk), lambda bi, hi, ci: (bi, ci, hi)),
            pl.BlockSpec((None, c, dk), lambda bi, hi, ci: (bi, ci, hi)),
            pl.BlockSpec((None, c, dv), lambda bi, hi, ci: (bi, ci, hi)),
            pl.BlockSpec((None, c, dv), lambda bi, hi, ci: (bi, ci, hi)),
            pl.BlockSpec((None, 1, dv), lambda bi, hi, ci: (hi, 0, 0)),
            pl.BlockSpec((None, None, dk, dv), lambda bi, hi, ci: (bi, hi, 0, 0)),
        ],
        out_specs=[
            pl.BlockSpec((None, c, dv), lambda bi, hi, ci: (bi, ci, hi)),
            pl.BlockSpec((None, None, dk, dv), lambda bi, hi, ci: (bi, hi, 0, 0)),
        ],
        out_shape=[jax.ShapeDtypeStruct((b, t, h * dv), BF16), jax.ShapeDtypeStruct((b, h, dk, dv), F32)],
        compiler_params=pltpu.CompilerParams(
            dimension_semantics=("parallel", "parallel", "arbitrary"), vmem_limit_bytes=VMEM_LIMIT_BYTES),
        name="ret_scan",
    )(log_gamma.reshape(h, 1, 1), q, k, v, gate, gn_gain.reshape(h, 1, dv), s0)


def _first_row(a):
    row = lax.broadcasted_iota(jnp.int32, (SUBLANES, a.shape[1]), 0)
    return jnp.where(row == 0, jnp.broadcast_to(a, (SUBLANES, a.shape[1])), 0.0)


def _gdn_step_kernel(q_ref, k_ref, v_ref, beta_ref, g_ref, z_ref, gn_ref, s0_ref, o_ref, s_ref, *, rep):
    hv = s0_ref.shape[0]
    for h in range(hv):
        j = h // rep
        q = q_ref[j:j + 1, :]
        k = k_ref[j:j + 1, :]
        v = v_ref[h:h + 1, :]
        beta = beta_ref[h:h + 1, :]
        eg = jnp.exp(g_ref[h:h + 1, :])
        s = s0_ref[h]
        kb = k * beta
        lhs = jnp.concatenate([kb * eg, q * eg, jnp.zeros((SUBLANES - 2, q.shape[1]), F32)], axis=0)
        both = jnp.dot(lhs, s, preferred_element_type=F32)
        v_new = v * beta - both[0:1, :]
        attn = jnp.sum(_bf(q) * _bf(k), axis=1, keepdims=True)
        o = both[1:2, :] + _bf(attn) * _bf(v_new)
        o_ref[h:h + 1, :] = _gated_rms(o, gn_ref[...], z_ref[h:h + 1, :])
        s_ref[h] = s * eg + lax.dot_general(_first_row(k), _first_row(v_new), (((0,), (0,)), ((), ())),
                                            preferred_element_type=F32)


def _gdn_step(q, k, v, beta, g, z, o_norm, s0):
    b, hq, dk = q.shape
    hv, dv = v.shape[1:]
    spec = lambda *shape: pl.BlockSpec((None,) + shape, lambda bi: (bi,) + (0,) * len(shape))
    return pl.pallas_call(
        functools.partial(_gdn_step_kernel, rep=hv // hq),
        grid=(b,),
        in_specs=[spec(hq, dk), spec(hq, dk), spec(hv, dv), spec(hv, 1), spec(hv, 1), spec(hv, dv),
                  pl.BlockSpec((1, dv), lambda bi: (0, 0)), spec(hv, dk, dv)],
        out_specs=[spec(hv, dv), spec(hv, dk, dv)],
        out_shape=[jax.ShapeDtypeStruct((b, hv, dv), F32), jax.ShapeDtypeStruct((b, hv, dk, dv), F32)],
        compiler_params=pltpu.CompilerParams(dimension_semantics=("parallel",), vmem_limit_bytes=VMEM_LIMIT_BYTES),
        name="gdn_step",
    )(q, k, v, beta[..., None], g[..., None], z, o_norm.reshape(1, dv), s0)


def _ret_step_kernel(lg_ref, q_ref, k_ref, v_ref, gate_ref, gn_ref, s0_ref, o_ref, s_ref):
    gamma = jnp.exp(lg_ref[...])
    q = q_ref[...]
    k = k_ref[...]
    v = v_ref[...]
    s = s0_ref[...]
    cross = jnp.dot(_first_row(q * gamma), s, preferred_element_type=F32)[0:1, :]
    attn = jnp.sum(_bf(q) * _bf(k), axis=1, keepdims=True)
    o_ref[...] = _gated_group_norm(_bf(attn) * _bf(v) + cross, gn_ref[...], gate_ref[...])
    s_ref[...] = s * gamma + lax.dot_general(_first_row(k), _first_row(v), (((0,), (0,)), ((), ())),
                                             preferred_element_type=F32)


def _ret_step(q, k, v, gate, gn_gain, s0, log_gamma):
    b, h, dk = q.shape
    dv = v.shape[2]
    vec = lambda n: pl.BlockSpec((None, None, 1, n), lambda bi, hi: (bi, hi, 0, 0))
    mat = pl.BlockSpec((None, None, dk, dv), lambda bi, hi: (bi, hi, 0, 0))
    o, s = pl.pallas_call(
        _ret_step_kernel,
        grid=(b, h),
        in_specs=[pl.BlockSpec((None, 1, 1), lambda bi, hi: (hi, 0, 0)), vec(dk), vec(dk), vec(dv), vec(dv),
                  pl.BlockSpec((None, 1, dv), lambda bi, hi: (hi, 0, 0)), mat],
        out_specs=[vec(dv), mat],
        out_shape=[jax.ShapeDtypeStruct((b, h, 1, dv), F32), jax.ShapeDtypeStruct((b, h, dk, dv), F32)],
        compiler_params=pltpu.CompilerParams(
            dimension_semantics=("parallel", "parallel"), vmem_limit_bytes=VMEM_LIMIT_BYTES),
        name="ret_step",
    )(log_gamma.reshape(h, 1, 1), q[:, :, None, :], k[:, :, None, :], v[:, :, None, :], gate[:, :, None, :],
      gn_gain.reshape(h, 1, dv), s0)
    return o[:, :, 0, :], s


def _l2_norm(x):
    return x * lax.rsqrt(jnp.sum(x * x, axis=-1, keepdims=True) + NORM_EPS)


def _fox_mixer(x, b, t, g, w_in, b_f, q_norm, k_norm, w_o, cache):
    d = D_MODEL
    prompt = cache is None
    scale = FOX_HEAD_DIM ** -0.5
    head_rms = dict(gain=g, epilogue="head_rms", head_dim=FOX_HEAD_DIM)
    kv_dtypes = (F32, BF16) if prompt else (F32,)
    k = _proj(x, w_in[:, d:2 * d], extra=(k_norm,), out_dtypes=kv_dtypes, **head_rms)
    v = _proj(x, w_in[:, 2 * d:3 * d], gain=g, out_dtypes=kv_dtypes)
    log_f = _proj(x, _pad_cols(w_in[:, 3 * d:]), gain=g, epilogue="log_sigmoid", extra=(_pad_cols(b_f),))
    log_f = log_f[:, :FOX_HEADS].reshape(b, t, FOX_HEADS)
    if prompt:
        (k, k_bf), (v, v_bf) = k, v
        q_bf = _proj(x, w_in[:, :d], extra=(q_norm,), out_dtypes=(BF16,), post_scale=scale * LOG2E, **head_rms)
        seq = lambda a: a.reshape(b, t, d)
        o = _fox_flash(seq(q_bf), seq(k_bf), seq(v_bf), jnp.cumsum(log_f, axis=1))
    else:
        assert t == 1
        q = _proj(x, w_in[:, :d], extra=(q_norm,), post_scale=scale, **head_rms)
        heads = lambda a: a.reshape(b, FOX_HEADS, FOX_HEAD_DIM)
        o = _fox_decode(heads(q), heads(k), heads(v), log_f[:, 0], *cache).astype(BF16)
    y = _proj(o.reshape(b * t, d), w_o, residual=x)
    hd = (b, t, FOX_HEADS, FOX_HEAD_DIM)
    return y, (k.reshape(hd), v.reshape(hd), log_f)


def _gdn_mixer(x, b, t, g, w_in, conv_w, a_log, dt_bias, o_norm, w_o, conv_buf, s0):
    n_main = GDN_CONV_DIM + GDN_V_DIM
    qkv = _proj(x, w_in[:, :GDN_CONV_DIM], gain=g)
    z = _proj(x, w_in[:, GDN_CONV_DIM:n_main], gain=g)
    tail = _proj(x, _pad_cols(w_in[:, n_main:]), gain=g)
    b_in = tail[:, :GDN_V_HEADS].reshape(b, t, GDN_V_HEADS)
    a_in = tail[:, GDN_V_HEADS:2 * GDN_V_HEADS].reshape(b, t, GDN_V_HEADS)
    beta = jax.nn.sigmoid(b_in)
    gg = -jnp.exp(a_log) * jax.nn.softplus(a_in + dt_bias)
    if t % GDN_CHUNK == 0:
        conv = functools.partial(_conv_silu, qkv, conv_buf, conv_w, seq_len=t)
        q = conv(0, GDN_QK_DIM, head_dim=GDN_DK, post_scale=GDN_DK ** -0.5)
        k = conv(GDN_QK_DIM, GDN_QK_DIM, head_dim=GDN_DK)
        v = conv(2 * GDN_QK_DIM, GDN_V_DIM)
        new_buf = qkv.reshape(b, t, GDN_CONV_DIM)[:, t - (GDN_CONV_WIDTH - 1):]
        seq = lambda a: a.reshape(b, t, a.shape[1])
        o, s_new = _gdn_scan(seq(q), seq(k), seq(v), beta, gg, seq(z), o_norm, s0)
    else:
        assert t == 1
        xc = jnp.concatenate([conv_buf, qkv[:, None, :]], axis=1)
        conv = xc[:, 0] * conv_w[0]
        for w in range(1, GDN_CONV_WIDTH):
            conv = conv + xc[:, w] * conv_w[w]
        conv = jax.nn.silu(conv)
        new_buf = xc[:, 1:]
        heads = lambda a: a.reshape(b, a.shape[1] // GDN_DK, GDN_DK)
        q = _l2_norm(heads(conv[:, :GDN_QK_DIM])) * GDN_DK ** -0.5
        k = _l2_norm(heads(conv[:, GDN_QK_DIM:2 * GDN_QK_DIM]))
        o, s_new = _gdn_step(q, k, heads(conv[:, 2 * GDN_QK_DIM:]), beta[:, 0], gg[:, 0], heads(z), o_norm, s0)
    y = _proj(o.reshape(b * t, GDN_V_DIM).astype(BF16), w_o, residual=x)
    return y, (new_buf, s_new)


def _ret_mixer(x, b, t, g, w_in, gn_gain, w_o, s0, pos0):
    d = D_MODEL
    half = RET_DK // 2
    inv = ROPE_BASE ** (-jnp.arange(half, dtype=F32) / half)
    ang = (pos0 + jnp.arange(t)).astype(F32)[:, None] * inv[None, :]
    rotary = dict(gain=g, epilogue="rotary", extra=(jnp.tile(jnp.cos(ang), (b, 1)), jnp.tile(jnp.sin(ang), (b, 1))),
                  head_dim=RET_DK)
    q = _proj(x, w_in[:, :d], **rotary)
    k = _proj(x, w_in[:, d:2 * d], post_scale=RET_DK ** -0.5, **rotary)
    v = _proj(x, w_in[:, 2 * d:2 * d + RET_V_DIM], gain=g)
    gate = _proj(x, w_in[:, 2 * d + RET_V_DIM:], gain=g)
    log_gamma = jnp.log1p(-jnp.exp2(-5.0 - jnp.arange(RET_HEADS, dtype=F32)))
    if t % RET_CHUNK == 0:
        seq = lambda a: a.reshape(b, t, a.shape[1])
        o, s_new = _ret_scan(seq(q), seq(k), seq(v), seq(gate), gn_gain, s0, log_gamma)
    else:
        assert t == 1
        heads = lambda a: a.reshape(b, RET_HEADS, a.shape[1] // RET_HEADS)
        o, s_new = _ret_step(heads(q), heads(k), heads(v), heads(gate), gn_gain, s0, log_gamma)
    y = _proj(o.reshape(b * t, RET_V_DIM).astype(BF16), w_o, residual=x)
    return y, (s_new,)


def kernel(x_prompt, x_sample, cache_k_l0, cache_v_l0, cache_logf_l0, state_conv_l1, state_gdn_l1, state_ret_l2, cache_k_l3, cache_v_l3, cache_logf_l3, page_table, norm_g, ffn_w_in, ffn_w_out, fox_w_in_l0, fox_b_f_l0, fox_q_norm_l0, fox_k_norm_l0, fox_w_o_l0, gdn_w_in_l1, gdn_conv_w_l1, gdn_a_log_l1, gdn_dt_bias_l1, gdn_o_norm_l1, gdn_w_o_l1, ret_w_in_l2, ret_gn_l2, ret_w_o_l2, fox_w_in_l3, fox_b_f_l3, fox_q_norm_l3, fox_k_norm_l3, fox_w_o_l3):
    ffn_in = ffn_w_in.astype(BF16)
    ffn_out = ffn_w_out.astype(BF16)
    fox0 = (fox_w_in_l0.astype(BF16), fox_b_f_l0, fox_q_norm_l0, fox_k_norm_l0, fox_w_o_l0.astype(BF16))
    fox3 = (fox_w_in_l3.astype(BF16), fox_b_f_l3, fox_q_norm_l3, fox_k_norm_l3, fox_w_o_l3.astype(BF16))
    gdn = (gdn_w_in_l1.astype(BF16), gdn_conv_w_l1, gdn_a_log_l1, gdn_dt_bias_l1, gdn_o_norm_l1,
           gdn_w_o_l1.astype(BF16))
    ret = (ret_w_in_l2.astype(BF16), ret_gn_l2, ret_w_o_l2.astype(BF16))

    def run(x3, sample):
        b, t, _ = x3.shape
        x = x3.reshape(b * t, D_MODEL)
        states = []
        for i in range(DEPTH):
            x = _ffn(x, norm_g[i, 0], ffn_in, ffn_out, i, 0)
            g = norm_g[i, 1]
            if i == 0:
                cache = (cache_k_l0, cache_v_l0, cache_logf_l0, page_table) if sample else None
                x, st = _fox_mixer(x, b, t, g, *fox0, cache)
            elif i == 1:
                if sample:
                    conv0, s0 = state_conv_l1, state_gdn_l1
                else:
                    conv0 = jnp.zeros((b, GDN_CONV_WIDTH - 1, GDN_CONV_DIM), F32)
                    s0 = jnp.zeros((b, GDN_V_HEADS, GDN_DK, GDN_DV), F32)
                x, st = _gdn_mixer(x, b, t, g, *gdn, conv0, s0)
            elif i == 2:
                s0 = state_ret_l2 if sample else jnp.zeros((b, RET_HEADS, RET_DK, RET_DV), F32)
                x, st = _ret_mixer(x, b, t, g, *ret, s0, PAST_LEN if sample else 0)
            else:
                cache = (cache_k_l3, cache_v_l3, cache_logf_l3, page_table) if sample else None
                x, st = _fox_mixer(x, b, t, g, *fox3, cache)
            x = _ffn(x, norm_g[i, 2], ffn_in, ffn_out, i, 1)
            states.append(st)
        return x.reshape(b, t, D_MODEL), states

    yp, st_p = run(x_prompt, False)
    ys, st_s = run(x_sample, True)
    new_state = []
    for sp, ss in zip(st_p, st_s):
        new_state += [*sp, *ss]
    return (yp, ys, *new_state)
```

```python
import functools

import jax
import jax.numpy as jnp
from jax import lax
from jax.experimental import pallas as pl
from jax.experimental.pallas import tpu as pltpu

F32 = jnp.float32
BF16 = jnp.bfloat16

D_MODEL = 2048
DEPTH = 4
PAST_LEN = 16384
D_FF = 5632
NORM_EPS = 1e-6
GN_EPS = 1e-5
FFN_HALF = 0.5

FOX_HEADS = 16
FOX_HEAD_DIM = D_MODEL // FOX_HEADS

GDN_QK_HEADS = 16
GDN_V_HEADS = 32
GDN_DK = 128
GDN_DV = 128
GDN_QK_DIM = GDN_QK_HEADS * GDN_DK
GDN_V_DIM = GDN_V_HEADS * GDN_DV
GDN_CONV_DIM = 2 * GDN_QK_DIM + GDN_V_DIM
GDN_CONV_WIDTH = 4
GDN_CHUNK = 64

RET_HEADS = 8
RET_DK = D_MODEL // RET_HEADS
RET_DV = 2 * RET_DK
RET_V_DIM = RET_HEADS * RET_DV
RET_CHUNK = 128
ROPE_BASE = 10000.0
LOG2E = 1.4426950408889634

LANES = 128
SUBLANES = 8
VMEM_LIMIT_BYTES = 56 * 1024 * 1024
ROW_TILE = 512
PROJ_ROW_TILE = 1024
COL_TILE = 1024
FF_TILE = 512
FOX_ATTN_TILE = 512
FOX_HEADS_PER_STEP = 2
FOX_PAGES_PER_STEP = 4
GDN_HEAD_GROUP = 16
RET_HEADS_PER_STEP = 2


def _row_tile(m, tile=ROW_TILE):
    return tile if m % tile == 0 else m


def _col_tile(n):
    for t in (COL_TILE, 768, 512, 256, 128):
        if n % t == 0:
            return t
    return n


def _dot(a, b):
    return jnp.dot(a.astype(BF16), b.astype(BF16), preferred_element_type=F32)


def _dot_nt(a, b):
    return lax.dot_general(a.astype(BF16), b.astype(BF16), (((1,), (1,)), ((), ())), preferred_element_type=F32)


def _dot_tn(a, b):
    return lax.dot_general(a.astype(BF16), b.astype(BF16), (((0,), (0,)), ((), ())), preferred_element_type=F32)


def _dot_f32(a, b):
    return jnp.dot(a, b, preferred_element_type=F32, precision=lax.Precision.HIGHEST)


def _bf(a):
    return a.astype(BF16).astype(F32)


def _gated_rms(o, gain, z):
    o = o * lax.rsqrt(jnp.mean(o * o, axis=-1, keepdims=True) + NORM_EPS) * gain
    return o * (z * jax.nn.sigmoid(z))


def _gated_group_norm(o, gain, gate):
    mu = jnp.mean(o, axis=-1, keepdims=True)
    var = jnp.mean(jnp.square(o - mu), axis=-1, keepdims=True)
    o = (o - mu) * lax.rsqrt(var + GN_EPS) * gain
    return (gate * jax.nn.sigmoid(gate)) * o


def _ffn_kernel(x_ref, g_ref, wg_ref, wu_ref, wo_ref, o_ref, *rest):
    *w_copies, xn_ref = rest
    j = pl.program_id(1)

    @pl.when(j == 0)
    def _():
        x = x_ref[...]
        inv = lax.rsqrt(jnp.mean(x * x, axis=-1, keepdims=True) + NORM_EPS)
        xn_ref[...] = (x * inv * g_ref[...]).astype(BF16)
        o_ref[...] = jnp.zeros_like(o_ref)

    wg, wu, wo = (r[...].astype(BF16) for r in (wg_ref, wu_ref, wo_ref))
    for dst, w in zip(w_copies, (wg, wu, wo)):
        dst[...] = w
    xn = xn_ref[...]
    gate = jnp.dot(xn, wg, preferred_element_type=F32)
    up = jnp.dot(xn, wu, preferred_element_type=F32)
    act = (gate * jax.nn.sigmoid(gate) * up).astype(BF16)
    o_ref[...] += jnp.dot(act, wo, preferred_element_type=F32)

    @pl.when(j == pl.num_programs(1) - 1)
    def _():
        o_ref[...] = x_ref[...] + FFN_HALF * o_ref[...]


def _ffn_weight_specs(d):
    tf = FF_TILE
    return [pl.BlockSpec((d, tf), lambda i, j: (0, j)), pl.BlockSpec((d, tf), lambda i, j: (0, j)),
            pl.BlockSpec((tf, d), lambda i, j: (j, 0))]


def _ffn_pallas(x, g, weights, w_specs, w_shapes, copies):
    m, d = x.shape
    tm = _row_tile(m)
    return pl.pallas_call(
        _ffn_kernel,
        grid=(m // tm, w_shapes[2][0] // FF_TILE),
        in_specs=[pl.BlockSpec((tm, d), lambda i, j: (i, 0)), pl.BlockSpec((1, d), lambda i, j: (0, 0))] + w_specs,
        out_specs=[pl.BlockSpec((tm, d), lambda i, j: (i, 0))] + (_ffn_weight_specs(d) if copies else []),
        out_shape=[jax.ShapeDtypeStruct((m, d), F32)]
        + ([jax.ShapeDtypeStruct(s, BF16) for s in w_shapes] if copies else []),
        scratch_shapes=[pltpu.VMEM((tm, d), BF16)],
        compiler_params=pltpu.CompilerParams(
            dimension_semantics=("parallel", "arbitrary"), vmem_limit_bytes=VMEM_LIMIT_BYTES),
        name="ffn",
    )(x, g.reshape(1, d), *weights)


def _ffn(x, g, weights):
    return _ffn_pallas(x, g, weights, _ffn_weight_specs(x.shape[1]), [w.shape for w in weights], copies=False)[0]


def _ffn_casting(x, g, w_in, w_out, layer, which):
    d, f = w_out.shape[3], w_out.shape[2]
    tf = FF_TILE
    nf = f // tf
    specs = [pl.BlockSpec((None, None, d, tf), lambda i, j: (layer, which, 0, j)),
             pl.BlockSpec((None, None, d, tf), lambda i, j: (layer, which, 0, j + nf)),
             pl.BlockSpec((None, None, tf, d), lambda i, j: (layer, which, j, 0))]
    y, *w_bf = _ffn_pallas(x, g, (w_in, w_in, w_out), specs, [(d, f), (d, f), (f, d)], copies=True)
    return y, tuple(w_bf)


def _proj_kernel(*refs, norm, residual, epilogue, n_extra, n_out, head_dim, post_scale):
    refs = list(refs)
    x_ref = refs.pop(0)
    g_ref = refs.pop(0) if norm else None
    w_ref = refs.pop(0)
    r_ref = refs.pop(0) if residual else None
    extra = [refs.pop(0) for _ in range(n_extra)]
    outs = [refs.pop(0) for _ in range(n_out)]
    if norm:
        xn_ref = refs.pop(0)

        @pl.when(pl.program_id(1) == 0)
        def _():
            x = x_ref[...]
            inv = lax.rsqrt(jnp.mean(x * x, axis=-1, keepdims=True) + NORM_EPS)
            xn_ref[...] = (x * inv * g_ref[...]).astype(BF16)

        lhs = xn_ref[...]
    else:
        lhs = x_ref[...]
    y = jnp.dot(lhs, w_ref[...], preferred_element_type=F32)
    if residual:
        y = r_ref[...] + y

    def emit(lo, hi, val):
        for o_ref in outs:
            o_ref[:, lo:hi] = val.astype(o_ref.dtype)

    tn = y.shape[1]
    if epilogue is None:
        emit(0, tn, y)
    elif epilogue == "head_rms":
        gain = extra[0][...]
        for lo in range(0, tn, head_dim):
            yh = y[:, lo:lo + head_dim]
            yh = yh * lax.rsqrt(jnp.mean(yh * yh, axis=-1, keepdims=True) + NORM_EPS) * gain
            emit(lo, lo + head_dim, yh if post_scale is None else yh * post_scale)
    elif epilogue == "rotary":
        cos = extra[0][...]
        sin = extra[1][...]
        half = head_dim // 2
        for lo in range(0, tn, head_dim):
            x1 = y[:, lo:lo + half]
            x2 = y[:, lo + half:lo + head_dim]
            o1 = x1 * cos - x2 * sin
            o2 = x1 * sin + x2 * cos
            emit(lo, lo + half, o1 if post_scale is None else o1 * post_scale)
            emit(lo + half, lo + head_dim, o2 if post_scale is None else o2 * post_scale)
    elif epilogue == "log_sigmoid":
        emit(0, tn, jax.nn.log_sigmoid(y + extra[0][...]))
    else:
        raise ValueError(epilogue)


def _proj(x, w, gain=None, residual=None, epilogue=None, extra=(), out_dtypes=(F32,), head_dim=None, post_scale=None):
    m, k = x.shape
    n = w.shape[1]
    tm = _row_tile(m, PROJ_ROW_TILE)
    tn = _col_tile(n)
    norm = gain is not None
    in_specs = [pl.BlockSpec((tm, k), lambda i, j: (i, 0))]
    args = [x]
    if norm:
        in_specs.append(pl.BlockSpec((1, k), lambda i, j: (0, 0)))
        args.append(gain.reshape(1, k))
    in_specs.append(pl.BlockSpec((k, tn), lambda i, j: (0, j)))
    args.append(w)
    if residual is not None:
        in_specs.append(pl.BlockSpec((tm, tn), lambda i, j: (i, j)))
        args.append(residual)
    if epilogue == "head_rms":
        in_specs.append(pl.BlockSpec((1, head_dim), lambda i, j: (0, 0)))
        args.append(extra[0].reshape(1, head_dim))
    elif epilogue == "rotary":
        for e in extra:
            in_specs.append(pl.BlockSpec((tm, head_dim // 2), lambda i, j: (i, 0)))
            args.append(e)
    elif epilogue == "log_sigmoid":
        in_specs.append(pl.BlockSpec((1, tn), lambda i, j: (0, j)))
        args.append(extra[0].reshape(1, n))
    outs = pl.pallas_call(
        functools.partial(_proj_kernel, norm=norm, residual=residual is not None, epilogue=epilogue,
                          n_extra=len(extra), n_out=len(out_dtypes), head_dim=head_dim, post_scale=post_scale),
        grid=(m // tm, n // tn),
        in_specs=in_specs,
        out_specs=[pl.BlockSpec((tm, tn), lambda i, j: (i, j)) for _ in out_dtypes],
        out_shape=[jax.ShapeDtypeStruct((m, n), dt) for dt in out_dtypes],
        scratch_shapes=[pltpu.VMEM((tm, k), BF16)] if norm else [],
        compiler_params=pltpu.CompilerParams(
            dimension_semantics=("parallel", "arbitrary"), vmem_limit_bytes=VMEM_LIMIT_BYTES),
        name="proj" if epilogue is None else "proj_" + epilogue,
    )(*args)
    return outs[0] if len(outs) == 1 else outs


def _pad_cols(w):
    return jnp.pad(w, ((0, 0),) * (w.ndim - 1) + ((0, LANES - w.shape[-1]),))


def _fox_flash_kernel(q_ref, k_ref, v_ref, ck_ref, o_ref, *, tile, heads):
    qi = pl.program_id(2)
    dh = q_ref.shape[1] // heads
    cols = [slice(h * dh, (h + 1) * dh) for h in range(heads)]
    qs = [q_ref[:, c] for c in cols]

    def step(j, carry, diagonal):
        ks = pl.multiple_of(j * tile, tile)
        ss = []
        for h, c in enumerate(cols):
            ck = ck_ref[h:h + 1, pl.ds(ks, tile)] * LOG2E
            s = lax.dot_general(qs[h], k_ref[pl.ds(ks, tile), c], (((1,), (1,)), ((), ())),
                                preferred_element_type=F32) - ck
            if diagonal:
                row = lax.broadcasted_iota(jnp.int32, s.shape, 0)
                col = lax.broadcasted_iota(jnp.int32, s.shape, 1)
                s = jnp.where(col <= row, s, -jnp.inf)
            ss.append(s)
        m_new = [jnp.maximum(m, jnp.max(s, axis=1, keepdims=True)) for (m, _, _), s in zip(carry, ss)]
        ps = [jnp.exp2(s - m) for s, m in zip(ss, m_new)]
        out = []
        for h, c in enumerate(cols):
            m, l, acc = carry[h]
            alpha = jnp.exp2(m - m_new[h])
            l = alpha * l + jnp.sum(ps[h], axis=1, keepdims=True)
            acc = alpha * acc + jnp.dot(ps[h].astype(BF16), v_ref[pl.ds(ks, tile), c], preferred_element_type=F32)
            out.append((m_new[h], l, acc))
        return tuple(out)

    init = tuple((jnp.full((tile, 1), -jnp.inf, F32), jnp.zeros((tile, 1), F32), jnp.zeros((tile, dh), F32))
                 for _ in cols)
    carry = lax.fori_loop(0, qi, functools.partial(step, diagonal=False), init)
    carry = step(qi, carry, True)
    for c, (m, l, acc) in zip(cols, carry):
        o_ref[:, c] = (acc / l).astype(o_ref.dtype)


def _fox_flash(q, k, v, c):
    b, s, d = q.shape
    h = c.shape[2]
    dh = d // h
    hp = FOX_HEADS_PER_STEP
    tile = min(FOX_ATTN_TILE, s)
    ct = jnp.swapaxes(c, 1, 2).reshape(b, h // hp, hp, s)
    return pl.pallas_call(
        functools.partial(_fox_flash_kernel, tile=tile, heads=hp),
        grid=(b, h // hp, s // tile),
        in_specs=[
            pl.BlockSpec((None, tile, hp * dh), lambda bi, hi, qi: (bi, qi, hi)),
            pl.BlockSpec((None, s, hp * dh), lambda bi, hi, qi: (bi, 0, hi)),
            pl.BlockSpec((None, s, hp * dh), lambda bi, hi, qi: (bi, 0, hi)),
            pl.BlockSpec((None, None, hp, s), lambda bi, hi, qi: (bi, hi, 0, 0)),
        ],
        out_specs=pl.BlockSpec((None, tile, hp * dh), lambda bi, hi, qi: (bi, qi, hi)),
        out_shape=jax.ShapeDtypeStruct((b, s, d), BF16),
        compiler_params=pltpu.CompilerParams(
            dimension_semantics=("parallel", "parallel", "arbitrary"), vmem_limit_bytes=VMEM_LIMIT_BYTES),
        name="fox_flash",
    )(q, k, v, ct)


def _fox_decode_kernel(pt_ref, q_ref, kn_ref, vn_ref, c0_ref, *refs, pages, heads):
    k_refs = refs[:pages]
    v_refs = refs[pages:2 * pages]
    lf_refs = refs[2 * pages:3 * pages]
    o_ref = refs[3 * pages]
    m_ref, l_ref, acc_ref, carry_ref = refs[3 * pages + 1:]
    p = pl.program_id(1)
    rows = k_refs[0].shape[0]
    width = lf_refs[0].shape[1]

    @pl.when(p == 0)
    def _():
        m_ref[...] = jnp.full_like(m_ref, -jnp.inf)
        l_ref[...] = jnp.zeros_like(l_ref)
        acc_ref[...] = jnp.zeros_like(acc_ref)
        carry_ref[...] = c0_ref[...]

    q = q_ref[...]
    head_of_row = lax.broadcasted_iota(jnp.int32, (heads, rows), 0)
    head_of_col = lax.broadcasted_iota(jnp.int32, (heads, rows), 1) % heads
    own = head_of_row == head_of_col
    lane = lax.broadcasted_iota(jnp.int32, (SUBLANES, width), 1)
    r0 = lax.broadcasted_iota(jnp.int32, (SUBLANES, SUBLANES), 0)
    r1 = lax.broadcasted_iota(jnp.int32, (SUBLANES, SUBLANES), 1)
    later_rows = (r1 > r0).astype(F32)
    all_rows = jnp.ones((SUBLANES, SUBLANES), F32)
    carry = carry_ref[...]
    scores = []
    for g in reversed(range(pages)):
        lf = lf_refs[g][...]
        scan = lf
        total = lf
        d = heads
        while d < width:
            scan = scan + jnp.where(lane + d < width, pltpu.roll(scan, width - d, axis=1), 0.0)
            total = total + pltpu.roll(total, width - d, axis=1)
            d *= 2
        bias = scan - lf + _dot_f32(later_rows, total) + carry
        carry = carry + _dot_f32(all_rows, total)
        bias = jnp.concatenate(
            [jnp.broadcast_to(bias[r:r + 1, :], (heads, width)) for r in range(SUBLANES)], axis=1)
        s = lax.dot_general(q, k_refs[g][...], (((1,), (1,)), ((), ())), preferred_element_type=F32)
        scores.append((g, jnp.where(own, s + bias, -jnp.inf)))
    carry_ref[...] = carry
    m = m_ref[...]
    m_new = m
    for _, s in scores:
        m_new = jnp.maximum(m_new, jnp.max(s, axis=1, keepdims=True))
    alpha = jnp.exp(m - m_new)
    l = alpha * l_ref[...]
    acc = alpha * acc_ref[...]
    for g, s in scores:
        pr = jnp.exp(s - m_new)
        l = l + jnp.sum(pr, axis=1, keepdims=True)
        acc = acc + jnp.dot(pr, v_refs[g][...], preferred_element_type=F32)
    m_ref[...] = m_new
    l_ref[...] = l
    acc_ref[...] = acc

    @pl.when(p == pl.num_programs(1) - 1)
    def _():
        s_new = jnp.sum(_bf(q) * _bf(kn_ref[...]), axis=1, keepdims=True)
        m = m_ref[...]
        m_new = jnp.maximum(m, s_new)
        alpha = jnp.exp(m - m_new)
        p_new = jnp.exp(s_new - m_new)
        l = alpha * l_ref[...] + p_new
        acc = alpha * acc_ref[...] + _bf(p_new) * _bf(vn_ref[...])
        o_ref[...] = acc / l


def _fox_decode(q, k_new, v_new, log_f_new, cache_k, cache_v, cache_logf, page_table):
    b, h, dh = q.shape
    n_pool, ps = cache_k.shape[:2]
    n_pages = page_table.shape[1]
    pages = FOX_PAGES_PER_STEP
    n_steps = n_pages // pages
    rows = ps * h
    width = rows // SUBLANES
    ck = cache_k.reshape(n_pool, rows, dh)
    cv = cache_v.reshape(n_pool, rows, dh)
    clf = cache_logf.reshape(n_pool, SUBLANES, width)
    c0 = jnp.broadcast_to(jnp.tile(log_f_new, (1, width // h))[:, None, :], (b, SUBLANES, width))

    def page_map(g):
        return lambda bi, p, pt: (pt[bi, (n_steps - 1 - p) * pages + g], 0, 0)

    row_spec = pl.BlockSpec((None, h, dh), lambda bi, p, pt: (bi, 0, 0))
    grid_spec = pltpu.PrefetchScalarGridSpec(
        num_scalar_prefetch=1,
        grid=(b, n_steps),
        in_specs=[row_spec, row_spec, row_spec,
                  pl.BlockSpec((None, SUBLANES, width), lambda bi, p, pt: (bi, 0, 0))]
        + [pl.BlockSpec((None, rows, dh), page_map(g)) for g in range(pages)]
        + [pl.BlockSpec((None, rows, dh), page_map(g)) for g in range(pages)]
        + [pl.BlockSpec((None, SUBLANES, width), page_map(g)) for g in range(pages)],
        out_specs=row_spec,
        scratch_shapes=[pltpu.VMEM((h, 1), F32), pltpu.VMEM((h, 1), F32), pltpu.VMEM((h, dh), F32),
                        pltpu.VMEM((SUBLANES, width), F32)],
    )
    return pl.pallas_call(
        functools.partial(_fox_decode_kernel, pages=pages, heads=h),
        grid_spec=grid_spec,
        out_shape=jax.ShapeDtypeStruct((b, h, dh), F32),
        compiler_params=pltpu.CompilerParams(
            dimension_semantics=("parallel", "arbitrary"), vmem_limit_bytes=VMEM_LIMIT_BYTES),
        name="fox_decode",
    )(page_table, q, k_new, v_new, c0, *([ck] * pages), *([cv] * pages), *([clf] * pages))


def _conv_silu_kernel(cur_ref, prev_ref, st_ref, w_ref, o_ref, *, tiles_per_seq, head_dim, post_scale):
    taps = w_ref.shape[0]
    cur = cur_ref[...]
    first = pl.program_id(0) % tiles_per_seq == 0
    prev = jnp.where(first, st_ref[...], prev_ref[...])
    w = w_ref[...]

    def finish(conv):
        y = conv * jax.nn.sigmoid(conv)
        if head_dim is None:
            return y
        parts = []
        for lo in range(0, y.shape[1], head_dim):
            yh = y[:, lo:lo + head_dim]
            yh = yh * lax.rsqrt(jnp.sum(yh * yh, axis=-1, keepdims=True) + NORM_EPS)
            parts.append(yh if post_scale is None else yh * post_scale)
        return jnp.concatenate(parts, axis=1)

    conv = cur * w[taps - 1:taps, :]
    for s in range(1, taps):
        conv = conv + pltpu.roll(cur, s, axis=0) * w[taps - 1 - s:taps - s, :]
    o_ref[...] = finish(conv)
    top = cur[0:SUBLANES, :]
    row = lax.broadcasted_iota(jnp.int32, top.shape, 0)
    conv = top * w[taps - 1:taps, :]
    for s in range(1, taps):
        shifted = jnp.where(row < s, pltpu.roll(prev, s, axis=0), pltpu.roll(top, s, axis=0))
        conv = conv + shifted * w[taps - 1 - s:taps - s, :]
    o_ref[0:SUBLANES, :] = finish(conv)


def _conv_silu(x, state, w, col0, ncols, seq_len, head_dim=None, post_scale=None):
    m, c = x.shape
    b = m // seq_len
    taps = w.shape[0]
    tm = min(ROW_TILE, seq_len)
    tn = min(COL_TILE, ncols)
    tiles_per_seq = seq_len // tm
    j0 = col0 // tn
    st = jnp.pad(state, ((0, 0), (SUBLANES - (taps - 1), 0), (0, 0)))
    return pl.pallas_call(
        functools.partial(_conv_silu_kernel, tiles_per_seq=tiles_per_seq, head_dim=head_dim, post_scale=post_scale),
        grid=(m // tm, ncols // tn),
        in_specs=[
            pl.BlockSpec((tm, tn), lambda i, j: (i, j + j0)),
            pl.BlockSpec((SUBLANES, tn), lambda i, j: (jnp.maximum(i * (tm // SUBLANES) - 1, 0), j + j0)),
            pl.BlockSpec((None, SUBLANES, tn), lambda i, j: (i // tiles_per_seq, 0, j + j0)),
            pl.BlockSpec((taps, tn), lambda i, j: (0, j + j0)),
        ],
        out_specs=pl.BlockSpec((tm, tn), lambda i, j: (i, j)),
        out_shape=jax.ShapeDtypeStruct((m, ncols), F32),
        compiler_params=pltpu.CompilerParams(
            dimension_semantics=("parallel", "parallel"), vmem_limit_bytes=VMEM_LIMIT_BYTES),
        name="conv_silu",
    )(x, x, st, w)


def _gdn_scan_kernel(q_ref, k_ref, v_ref, bcol_ref, gcol_ref, grow_ref, z_ref, gn_ref, s0_ref, o_ref, s_ref, *,
                     rep):
    c = q_ref.shape[0]
    dk = s_ref.shape[1]
    dv = s_ref.shape[2]
    hg = s_ref.shape[0]

    @pl.when(pl.program_id(2) == 0)
    def _():
        s_ref[...] = s0_ref[...]

    row = lax.broadcasted_iota(jnp.int32, (c, c), 0)
    col = lax.broadcasted_iota(jnp.int32, (c, c), 1)
    tril = row >= col
    strict = row > col
    g_cols = _dot_f32(tril.astype(F32), gcol_ref[...])
    g_rows = _dot_f32(grow_ref[...], (row <= col).astype(F32))
    heads = []
    for h in range(hg):
        j = h // rep
        qh = q_ref[:, j * dk:(j + 1) * dk]
        kh = k_ref[:, j * dk:(j + 1) * dk]
        gc = g_cols[:, h:h + 1]
        gr = g_rows[h:h + 1, :]
        decay = jnp.where(tril, jnp.exp(jnp.where(tril, gc - gr, 0.0)), 0.0)
        beta = bcol_ref[:, h:h + 1]
        kb = kh * beta
        exp_g = jnp.exp(gc)
        low = jnp.where(strict, _dot_nt(kb, kh) * decay, 0.0)
        attn = jnp.where(tril, _dot_nt(qh, kh) * decay, 0.0)
        x = jnp.concatenate([v_ref[:, h * dv:(h + 1) * dv] * beta, kb * exp_g], axis=1)
        heads.append(dict(qh=qh, kh=kh, gc=gc, exp_g=exp_g, low=low, attn=attn, x=x))
    for hd in heads:
        hd["x"] = hd["x"] - _dot(hd["low"], hd["x"])
        hd["p"] = hd["low"]
    n = 2
    while n < c:
        for hd in heads:
            hd["p"] = _dot(hd["p"], hd["p"])
        for hd in heads:
            hd["x"] = hd["x"] + _dot(hd["p"], hd["x"])
        n *= 2
    for h, hd in enumerate(heads):
        hd["s"] = s_ref[h]
        hd["v_new"] = hd["x"][:, :dv] - _dot(hd["x"][:, dv:], hd["s"])
    for h, hd in enumerate(heads):
        o = _dot(hd["qh"] * hd["exp_g"], hd["s"]) + _dot(hd["attn"], hd["v_new"])
        o_ref[:, h * dv:(h + 1) * dv] = _gated_rms(o, gn_ref[...], z_ref[:, h * dv:(h + 1) * dv]).astype(o_ref.dtype)
    for h, hd in enumerate(heads):
        gc = hd["gc"]
        g_last = gc[c - 1:c, :]
        s_ref[h] = hd["s"] * jnp.exp(g_last) + _dot_tn(hd["kh"] * jnp.exp(g_last - gc), hd["v_new"])


def _gdn_scan(q, k, v, beta, g, z, o_norm, s0):
    b, t, _ = q.shape
    hv, dk, dv = s0.shape[1:]
    hq = q.shape[2] // dk
    rep = hv // hq
    c = GDN_CHUNK
    nc = t // c
    hg = min(GDN_HEAD_GROUP, hv)
    ng = hv // hg
    cols = lambda a: a.reshape(b, t, ng, hg).transpose(0, 2, 1, 3)
    g_rows = g.reshape(b, nc, c, hv).transpose(0, 1, 3, 2)
    o, s = pl.pallas_call(
        functools.partial(_gdn_scan_kernel, rep=rep),
        grid=(b, ng, nc),
        in_specs=[
            pl.BlockSpec((None, c, hg // rep * dk), lambda bi, gi, ci: (bi, ci, gi)),
            pl.BlockSpec((None, c, hg // rep * dk), lambda bi, gi, ci: (bi, ci, gi)),
            pl.BlockSpec((None, c, hg * dv), lambda bi, gi, ci: (bi, ci, gi)),
            pl.BlockSpec((None, None, c, hg), lambda bi, gi, ci: (bi, gi, ci, 0)),
            pl.BlockSpec((None, None, c, hg), lambda bi, gi, ci: (bi, gi, ci, 0)),
            pl.BlockSpec((None, None, hg, c), lambda bi, gi, ci: (bi, ci, gi, 0)),
            pl.BlockSpec((None, c, hg * dv), lambda bi, gi, ci: (bi, ci, gi)),
            pl.BlockSpec((1, dv), lambda bi, gi, ci: (0, 0)),
            pl.BlockSpec((None, hg, dk, dv), lambda bi, gi, ci: (bi, gi, 0, 0)),
        ],
        out_specs=[
            pl.BlockSpec((None, c, hg * dv), lambda bi, gi, ci: (bi, ci, gi)),
            pl.BlockSpec((None, hg, dk, dv), lambda bi, gi, ci: (bi, gi, 0, 0)),
        ],
        out_shape=[jax.ShapeDtypeStruct((b, t, hv * dv), BF16), jax.ShapeDtypeStruct((b, hv, dk, dv), F32)],
        compiler_params=pltpu.CompilerParams(
            dimension_semantics=("parallel", "parallel", "arbitrary"), vmem_limit_bytes=VMEM_LIMIT_BYTES),
        name="gdn_scan",
    )(q, k, v, cols(beta), cols(g), g_rows, z, o_norm.reshape(1, dv), s0)
    return o, s


def _ret_scan_kernel(lg_ref, q_ref, k_ref, v_ref, gate_ref, gn_ref, s0_ref, o_ref, s_ref):
    c = q_ref.shape[0]
    heads, dk, dv = s_ref.shape

    @pl.when(pl.program_id(2) == 0)
    def _():
        s_ref[...] = s0_ref[...]

    row = lax.broadcasted_iota(jnp.int32, (c, c), 0)
    col = lax.broadcasted_iota(jnp.int32, (c, c), 1)
    tril = row >= col
    dist = jnp.where(tril, row - col, 0).astype(F32)
    pos = lax.broadcasted_iota(jnp.int32, (c, 1), 0).astype(F32)
    hs = []
    for h in range(heads):
        lg = lg_ref[h]
        q = q_ref[:, h * dk:(h + 1) * dk]
        k = k_ref[:, h * dk:(h + 1) * dk]
        qk = lax.dot_general(q, k, (((1,), (1,)), ((), ())), preferred_element_type=F32)
        hs.append(dict(lg=lg, q=q, k=k, qk=qk * jnp.where(tril, jnp.exp(lg * dist), 0.0), s=s_ref[h]))
    for h, hd in enumerate(hs):
        v = v_ref[:, h * dv:(h + 1) * dv]
        inner = jnp.dot(hd["qk"], v, preferred_element_type=F32)
        cross = jnp.dot(hd["q"] * jnp.exp(hd["lg"] * (pos + 1.0)), hd["s"], preferred_element_type=F32)
        gate = gate_ref[:, h * dv:(h + 1) * dv]
        o_ref[:, h * dv:(h + 1) * dv] = _gated_group_norm(inner + cross, gn_ref[h], gate).astype(o_ref.dtype)
    for h, hd in enumerate(hs):
        lg = hd["lg"]
        kd = hd["k"] * jnp.exp(lg * (c - 1.0 - pos))
        s_ref[h] = hd["s"] * jnp.exp(lg * c) + lax.dot_general(
            kd, v_ref[:, h * dv:(h + 1) * dv], (((0,), (0,)), ((), ())), preferred_element_type=F32)


def _ret_scan(q, k, v, gate, gn_gain, s0, log_gamma):
    b, t, _ = q.shape
    h, dk, dv = s0.shape[1:]
    c = RET_CHUNK
    hp = RET_HEADS_PER_STEP
    seq = lambda width: pl.BlockSpec((None, c, hp * width), lambda bi, hi, ci: (bi, ci, hi))
    state = pl.BlockSpec((None, hp, dk, dv), lambda bi, hi, ci: (bi, hi, 0, 0))
    return pl.pallas_call(
        _ret_scan_kernel,
        grid=(b, h // hp, t // c),
        in_specs=[
            pl.BlockSpec((hp, 1, 1), lambda bi, hi, ci: (hi, 0, 0)),
            seq(dk), seq(dk), seq(dv), seq(dv),
            pl.BlockSpec((hp, 1, dv), lambda bi, hi, ci: (hi, 0, 0)),
            state,
        ],
        out_specs=[seq(dv), state],
        out_shape=[jax.ShapeDtypeStruct((b, t, h * dv), BF16), jax.ShapeDtypeStruct((b, h, dk, dv), F32)],
        compiler_params=pltpu.CompilerParams(
            dimension_semantics=("parallel", "parallel", "arbitrary"), vmem_limit_bytes=VMEM_LIMIT_BYTES),
        name="ret_scan",
    )(log_gamma.reshape(h, 1, 1), q, k, v, gate, gn_gain.reshape(h, 1, dv), s0)


def _first_row(a):
    row = lax.broadcasted_iota(jnp.int32, (SUBLANES, a.shape[1]), 0)
    return jnp.where(row == 0, jnp.broadcast_to(a, (SUBLANES, a.shape[1])), 0.0)


def _gdn_step_kernel(q_ref, k_ref, v_ref, beta_ref, g_ref, z_ref, gn_ref, s0_ref, o_ref, s_ref, *, rep):
    hv = s0_ref.shape[0]
    for h in range(hv):
        j = h // rep
        q = q_ref[j:j + 1, :]
        k = k_ref[j:j + 1, :]
        v = v_ref[h:h + 1, :]
        beta = beta_ref[h:h + 1, :]
        eg = jnp.exp(g_ref[h:h + 1, :])
        s = s0_ref[h]
        kb = k * beta
        lhs = jnp.concatenate([kb * eg, q * eg, jnp.zeros((SUBLANES - 2, q.shape[1]), F32)], axis=0)
        both = jnp.dot(lhs, s, preferred_element_type=F32)
        v_new = v * beta - both[0:1, :]
        attn = jnp.sum(_bf(q) * _bf(k), axis=1, keepdims=True)
        o = both[1:2, :] + _bf(attn) * _bf(v_new)
        o_ref[h:h + 1, :] = _gated_rms(o, gn_ref[...], z_ref[h:h + 1, :])
        s_ref[h] = s * eg + lax.dot_general(_first_row(k), _first_row(v_new), (((0,), (0,)), ((), ())),
                                            preferred_element_type=F32)


def _gdn_step(q, k, v, beta, g, z, o_norm, s0):
    b, hq, dk = q.shape
    hv, dv = v.shape[1:]
    spec = lambda *shape: pl.BlockSpec((None,) + shape, lambda bi: (bi,) + (0,) * len(shape))
    return pl.pallas_call(
        functools.partial(_gdn_step_kernel, rep=hv // hq),
        grid=(b,),
        in_specs=[spec(hq, dk), spec(hq, dk), spec(hv, dv), spec(hv, 1), spec(hv, 1), spec(hv, dv),
                  pl.BlockSpec((1, dv), lambda bi: (0, 0)), spec(hv, dk, dv)],
        out_specs=[spec(hv, dv), spec(hv, dk, dv)],
        out_shape=[jax.ShapeDtypeStruct((b, hv, dv), F32), jax.ShapeDtypeStruct((b, hv, dk, dv), F32)],
        compiler_params=pltpu.CompilerParams(dimension_semantics=("parallel",), vmem_limit_bytes=VMEM_LIMIT_BYTES),
        name="gdn_step",
    )(q, k, v, beta[..., None], g[..., None], z, o_norm.reshape(1, dv), s0)


def _ret_step_kernel(lg_ref, q_ref, k_ref, v_ref, gate_ref, gn_ref, s0_ref, o_ref, s_ref):
    gamma = jnp.exp(lg_ref[...])
    q = q_ref[...]
    k = k_ref[...]
    v = v_ref[...]
    s = s0_ref[...]
    cross = jnp.dot(_first_row(q * gamma), s, preferred_element_type=F32)[0:1, :]
    attn = jnp.sum(_bf(q) * _bf(k), axis=1, keepdims=True)
    o_ref[...] = _gated_group_norm(_bf(attn) * _bf(v) + cross, gn_ref[...], gate_ref[...])
    s_ref[...] = s * gamma + lax.dot_general(_first_row(k), _first_row(v), (((0,), (0,)), ((), ())),
                                             preferred_element_type=F32)


def _ret_step(q, k, v, gate, gn_gain, s0, log_gamma):
    b, h, dk = q.shape
    dv = v.shape[2]
    vec = lambda n: pl.BlockSpec((None, None, 1, n), lambda bi, hi: (bi, hi, 0, 0))
    mat = pl.BlockSpec((None, None, dk, dv), lambda bi, hi: (bi, hi, 0, 0))
    o, s = pl.pallas_call(
        _ret_step_kernel,
        grid=(b, h),
        in_specs=[pl.BlockSpec((None, 1, 1), lambda bi, hi: (hi, 0, 0)), vec(dk), vec(dk), vec(dv), vec(dv),
                  pl.BlockSpec((None, 1, dv), lambda bi, hi: (hi, 0, 0)), mat],
        out_specs=[vec(dv), mat],
        out_shape=[jax.ShapeDtypeStruct((b, h, 1, dv), F32), jax.ShapeDtypeStruct((b, h, dk, dv), F32)],
        compiler_params=pltpu.CompilerParams(
            dimension_semantics=("parallel", "parallel"), vmem_limit_bytes=VMEM_LIMIT_BYTES),
        name="ret_step",
    )(log_gamma.reshape(h, 1, 1), q[:, :, None, :], k[:, :, None, :], v[:, :, None, :], gate[:, :, None, :],
      gn_gain.reshape(h, 1, dv), s0)
    return o[:, :, 0, :], s


def _l2_norm(x):
    return x * lax.rsqrt(jnp.sum(x * x, axis=-1, keepdims=True) + NORM_EPS)


def _fox_mixer(x, b, t, g, w_in, b_f, q_norm, k_norm, w_o, cache):
    d = D_MODEL
    prompt = cache is None
    scale = FOX_HEAD_DIM ** -0.5
    head_rms = dict(gain=g, epilogue="head_rms", head_dim=FOX_HEAD_DIM)
    kv_dtypes = (F32, BF16) if prompt else (F32,)
    k = _proj(x, w_in[:, d:2 * d], extra=(k_norm,), out_dtypes=kv_dtypes, **head_rms)
    v = _proj(x, w_in[:, 2 * d:3 * d], gain=g, out_dtypes=kv_dtypes)
    log_f = _proj(x, _pad_cols(w_in[:, 3 * d:]), gain=g, epilogue="log_sigmoid", extra=(_pad_cols(b_f),))
    log_f = log_f[:, :FOX_HEADS].reshape(b, t, FOX_HEADS)
    if prompt:
        (k, k_bf), (v, v_bf) = k, v
        q_bf = _proj(x, w_in[:, :d], extra=(q_norm,), out_dtypes=(BF16,), post_scale=scale * LOG2E, **head_rms)
        seq = lambda a: a.reshape(b, t, d)
        o = _fox_flash(seq(q_bf), seq(k_bf), seq(v_bf), jnp.cumsum(log_f, axis=1))
    else:
        assert t == 1
        q = _proj(x, w_in[:, :d], extra=(q_norm,), post_scale=scale, **head_rms)
        heads = lambda a: a.reshape(b, FOX_HEADS, FOX_HEAD_DIM)
        o = _fox_decode(heads(q), heads(k), heads(v), log_f[:, 0], *cache).astype(BF16)
    y = _proj(o.reshape(b * t, d), w_o, residual=x)
    hd = (b, t, FOX_HEADS, FOX_HEAD_DIM)
    return y, (k.reshape(hd), v.reshape(hd), log_f)


def _gdn_mixer(x, b, t, g, w_in, conv_w, a_log, dt_bias, o_norm, w_o, conv_buf, s0):
    n_main = GDN_CONV_DIM + GDN_V_DIM
    qkv = _proj(x, w_in[:, :GDN_CONV_DIM], gain=g)
    z = _proj(x, w_in[:, GDN_CONV_DIM:n_main], gain=g)
    tail = _proj(x, _pad_cols(w_in[:, n_main:]), gain=g)
    b_in = tail[:, :GDN_V_HEADS].reshape(b, t, GDN_V_HEADS)
    a_in = tail[:, GDN_V_HEADS:2 * GDN_V_HEADS].reshape(b, t, GDN_V_HEADS)
    beta = jax.nn.sigmoid(b_in)
    gg = -jnp.exp(a_log) * jax.nn.softplus(a_in + dt_bias)
    if t % GDN_CHUNK == 0:
        conv = functools.partial(_conv_silu, qkv, conv_buf, conv_w, seq_len=t)
        q = conv(0, GDN_QK_DIM, head_dim=GDN_DK, post_scale=GDN_DK ** -0.5)
        k = conv(GDN_QK_DIM, GDN_QK_DIM, head_dim=GDN_DK)
        v = conv(2 * GDN_QK_DIM, GDN_V_DIM)
        new_buf = qkv.reshape(b, t, GDN_CONV_DIM)[:, t - (GDN_CONV_WIDTH - 1):]
        seq = lambda a: a.reshape(b, t, a.shape[1])
        o, s_new = _gdn_scan(seq(q), seq(k), seq(v), beta, gg, seq(z), o_norm, s0)
    else:
        assert t == 1
        xc = jnp.concatenate([conv_buf, qkv[:, None, :]], axis=1)
        conv = xc[:, 0] * conv_w[0]
        for w in range(1, GDN_CONV_WIDTH):
            conv = conv + xc[:, w] * conv_w[w]
        conv = jax.nn.silu(conv)
        new_buf = xc[:, 1:]
        heads = lambda a: a.reshape(b, a.shape[1] // GDN_DK, GDN_DK)
        q = _l2_norm(heads(conv[:, :GDN_QK_DIM])) * GDN_DK ** -0.5
        k = _l2_norm(heads(conv[:, GDN_QK_DIM:2 * GDN_QK_DIM]))
        o, s_new = _gdn_step(q, k, heads(conv[:, 2 * GDN_QK_DIM:]), beta[:, 0], gg[:, 0], heads(z), o_norm, s0)
    y = _proj(o.reshape(b * t, GDN_V_DIM).astype(BF16), w_o, residual=x)
    return y, (new_buf, s_new)


def _ret_mixer(x, b, t, g, w_in, gn_gain, w_o, s0, pos0):
    d = D_MODEL
    half = RET_DK // 2
    inv = ROPE_BASE ** (-jnp.arange(half, dtype=F32) / half)
    ang = (pos0 + jnp.arange(t)).astype(F32)[:, None] * inv[None, :]
    rotary = dict(gain=g, epilogue="rotary", extra=(jnp.tile(jnp.cos(ang), (b, 1)), jnp.tile(jnp.sin(ang), (b, 1))),
                  head_dim=RET_DK)
    q = _proj(x, w_in[:, :d], **rotary)
    k = _proj(x, w_in[:, d:2 * d], post_scale=RET_DK ** -0.5, **rotary)
    v = _proj(x, w_in[:, 2 * d:2 * d + RET_V_DIM], gain=g)
    gate = _proj(x, w_in[:, 2 * d + RET_V_DIM:], gain=g)
    log_gamma = jnp.log1p(-jnp.exp2(-5.0 - jnp.arange(RET_HEADS, dtype=F32)))
    if t % RET_CHUNK == 0:
        seq = lambda a: a.reshape(b, t, a.shape[1])
        o, s_new = _ret_scan(seq(q), seq(k), seq(v), seq(gate), gn_gain, s0, log_gamma)
    else:
        assert t == 1
        heads = lambda a: a.reshape(b, RET_HEADS, a.shape[1] // RET_HEADS)
        o, s_new = _ret_step(heads(q), heads(k), heads(v), heads(gate), gn_gain, s0, log_gamma)
    y = _proj(o.reshape(b * t, RET_V_DIM).astype(BF16), w_o, residual=x)
    return y, (s_new,)


def kernel(x_prompt, x_sample, cache_k_l0, cache_v_l0, cache_logf_l0, state_conv_l1, state_gdn_l1, state_ret_l2, cache_k_l3, cache_v_l3, cache_logf_l3, page_table, norm_g, ffn_w_in, ffn_w_out, fox_w_in_l0, fox_b_f_l0, fox_q_norm_l0, fox_k_norm_l0, fox_w_o_l0, gdn_w_in_l1, gdn_conv_w_l1, gdn_a_log_l1, gdn_dt_bias_l1, gdn_o_norm_l1, gdn_w_o_l1, ret_w_in_l2, ret_gn_l2, ret_w_o_l2, fox_w_in_l3, fox_b_f_l3, fox_q_norm_l3, fox_k_norm_l3, fox_w_o_l3):
    ffn_bf16 = {}
    fox0 = (fox_w_in_l0.astype(BF16), fox_b_f_l0, fox_q_norm_l0, fox_k_norm_l0, fox_w_o_l0.astype(BF16))
    fox3 = (fox_w_in_l3.astype(BF16), fox_b_f_l3, fox_q_norm_l3, fox_k_norm_l3, fox_w_o_l3.astype(BF16))
    gdn = (gdn_w_in_l1.astype(BF16), gdn_conv_w_l1, gdn_a_log_l1, gdn_dt_bias_l1, gdn_o_norm_l1,
           gdn_w_o_l1.astype(BF16))
    ret = (ret_w_in_l2.astype(BF16), ret_gn_l2, ret_w_o_l2.astype(BF16))

    def ffn(x, sample, i, which):
        g = norm_g[i, 2 * which]
        if sample:
            x, ffn_bf16[i, which] = _ffn_casting(x, g, ffn_w_in, ffn_w_out, i, which)
            return x
        return _ffn(x, g, ffn_bf16[i, which])

    def run(x3, sample):
        b, t, _ = x3.shape
        x = x3.reshape(b * t, D_MODEL)
        states = []
        for i in range(DEPTH):
            x = ffn(x, sample, i, 0)
            g = norm_g[i, 1]
            if i == 0:
                cache = (cache_k_l0, cache_v_l0, cache_logf_l0, page_table) if sample else None
                x, st = _fox_mixer(x, b, t, g, *fox0, cache)
            elif i == 1:
                if sample:
                    conv0, s0 = state_conv_l1, state_gdn_l1
                else:
                    conv0 = jnp.zeros((b, GDN_CONV_WIDTH - 1, GDN_CONV_DIM), F32)
                    s0 = jnp.zeros((b, GDN_V_HEADS, GDN_DK, GDN_DV), F32)
                x, st = _gdn_mixer(x, b, t, g, *gdn, conv0, s0)
            elif i == 2:
                s0 = state_ret_l2 if sample else jnp.zeros((b, RET_HEADS, RET_DK, RET_DV), F32)
                x, st = _ret_mixer(x, b, t, g, *ret, s0, PAST_LEN if sample else 0)
            else:
                cache = (cache_k_l3, cache_v_l3, cache_logf_l3, page_table) if sample else None
                x, st = _fox_mixer(x, b, t, g, *fox3, cache)
            x = ffn(x, sample, i, 1)
            states.append(st)
        return x.reshape(b, t, D_MODEL), states

    ys, st_s = run(x_sample, True)
    yp, st_p = run(x_prompt, False)
    new_state = []
    for sp, ss in zip(st_p, st_s):
        new_state += [*sp, *ss]
    return (yp, ys, *new_state)
```

```python
import functools

import jax
import jax.numpy as jnp
from jax import lax
from jax.experimental import pallas as pl
from jax.experimental.pallas import tpu as pltpu

F32 = jnp.float32
BF16 = jnp.bfloat16

D_MODEL = 2048
DEPTH = 4
PAST_LEN = 16384
D_FF = 5632
NORM_EPS = 1e-6
GN_EPS = 1e-5
FFN_HALF = 0.5

FOX_HEADS = 16
FOX_HEAD_DIM = D_MODEL // FOX_HEADS

GDN_QK_HEADS = 16
GDN_V_HEADS = 32
GDN_DK = 128
GDN_DV = 128
GDN_QK_DIM = GDN_QK_HEADS * GDN_DK
GDN_V_DIM = GDN_V_HEADS * GDN_DV
GDN_CONV_DIM = 2 * GDN_QK_DIM + GDN_V_DIM
GDN_CONV_WIDTH = 4
GDN_CHUNK = 64

RET_HEADS = 8
RET_DK = D_MODEL // RET_HEADS
RET_DV = 2 * RET_DK
RET_V_DIM = RET_HEADS * RET_DV
RET_CHUNK = 128
ROPE_BASE = 10000.0
LOG2E = 1.4426950408889634

LANES = 128
SUBLANES = 8
VMEM_LIMIT_BYTES = 56 * 1024 * 1024
ROW_TILE = 512
PROJ_ROW_TILE = 1024
COL_TILE = 1024
FF_TILE = 512
FOX_ATTN_TILE = 512
FOX_HEADS_PER_STEP = 2
FOX_PAGES_PER_STEP = 8
GDN_HEAD_GROUP = 32
RET_HEADS_PER_STEP = 4


def _row_tile(m, tile=ROW_TILE):
    return tile if m % tile == 0 else m


def _col_tile(n):
    for t in (COL_TILE, 768, 512, 256, 128):
        if n % t == 0:
            return t
    return n


def _dot(a, b):
    return jnp.dot(a.astype(BF16), b.astype(BF16), preferred_element_type=F32)


def _dot_nt(a, b):
    return lax.dot_general(a.astype(BF16), b.astype(BF16), (((1,), (1,)), ((), ())), preferred_element_type=F32)


def _dot_tn(a, b):
    return lax.dot_general(a.astype(BF16), b.astype(BF16), (((0,), (0,)), ((), ())), preferred_element_type=F32)


def _dot_f32(a, b):
    return jnp.dot(a, b, preferred_element_type=F32, precision=lax.Precision.HIGHEST)


def _bf(a):
    return a.astype(BF16).astype(F32)


def _gated_rms(o, gain, z):
    o = o * lax.rsqrt(jnp.mean(o * o, axis=-1, keepdims=True) + NORM_EPS) * gain
    return o * (z * jax.nn.sigmoid(z))


def _gated_group_norm(o, gain, gate):
    mu = jnp.mean(o, axis=-1, keepdims=True)
    var = jnp.mean(jnp.square(o - mu), axis=-1, keepdims=True)
    o = (o - mu) * lax.rsqrt(var + GN_EPS) * gain
    return (gate * jax.nn.sigmoid(gate)) * o


def _ffn_kernel(x_ref, g_ref, wg_ref, wu_ref, wo_ref, o_ref, *rest):
    *w_copies, xn_ref = rest
    j = pl.program_id(1)

    @pl.when(j == 0)
    def _():
        x = x_ref[...]
        inv = lax.rsqrt(jnp.mean(x * x, axis=-1, keepdims=True) + NORM_EPS)
        xn_ref[...] = (x * inv * g_ref[...]).astype(BF16)
        o_ref[...] = jnp.zeros_like(o_ref)

    wg, wu, wo = (r[...].astype(BF16) for r in (wg_ref, wu_ref, wo_ref))
    for dst, w in zip(w_copies, (wg, wu, wo)):
        dst[...] = w
    xn = xn_ref[...]
    gate = jnp.dot(xn, wg, preferred_element_type=F32)
    up = jnp.dot(xn, wu, preferred_element_type=F32)
    act = (gate * jax.nn.sigmoid(gate) * up).astype(BF16)
    o_ref[...] += jnp.dot(act, wo, preferred_element_type=F32)

    @pl.when(j == pl.num_programs(1) - 1)
    def _():
        o_ref[...] = x_ref[...] + FFN_HALF * o_ref[...]


def _ffn_weight_specs(d):
    tf = FF_TILE
    return [pl.BlockSpec((None, d, tf), lambda i, j: (j, 0, 0)), pl.BlockSpec((None, d, tf), lambda i, j: (j, 0, 0)),
            pl.BlockSpec((tf, d), lambda i, j: (j, 0))]


def _ffn_pallas(x, g, weights, w_specs, w_shapes, copies):
    m, d = x.shape
    tm = _row_tile(m)
    return pl.pallas_call(
        _ffn_kernel,
        grid=(m // tm, w_shapes[2][0] // FF_TILE),
        in_specs=[pl.BlockSpec((tm, d), lambda i, j: (i, 0)), pl.BlockSpec((1, d), lambda i, j: (0, 0))] + w_specs,
        out_specs=[pl.BlockSpec((tm, d), lambda i, j: (i, 0))] + (_ffn_weight_specs(d) if copies else []),
        out_shape=[jax.ShapeDtypeStruct((m, d), F32)]
        + ([jax.ShapeDtypeStruct(s, BF16) for s in w_shapes] if copies else []),
        scratch_shapes=[pltpu.VMEM((tm, d), BF16)],
        compiler_params=pltpu.CompilerParams(
            dimension_semantics=("parallel", "arbitrary"), vmem_limit_bytes=VMEM_LIMIT_BYTES),
        name="ffn",
    )(x, g.reshape(1, d), *weights)


def _ffn(x, g, weights):
    return _ffn_pallas(x, g, weights, _ffn_weight_specs(x.shape[1]), [w.shape for w in weights], copies=False)[0]


def _ffn_casting(x, g, w_in, w_out, layer, which):
    d, f = w_out.shape[3], w_out.shape[2]
    tf = FF_TILE
    nf = f // tf
    specs = [pl.BlockSpec((None, None, d, tf), lambda i, j: (layer, which, 0, j)),
             pl.BlockSpec((None, None, d, tf), lambda i, j: (layer, which, 0, j + nf)),
             pl.BlockSpec((None, None, tf, d), lambda i, j: (layer, which, j, 0))]
    y, *w_bf = _ffn_pallas(x, g, (w_in, w_in, w_out), specs, [(nf, d, tf), (nf, d, tf), (f, d)], copies=True)
    return y, tuple(w_bf)


def _proj_kernel(*refs, norm, residual, epilogue, n_extra, n_out, head_dim, post_scale):
    refs = list(refs)
    x_ref = refs.pop(0)
    g_ref = refs.pop(0) if norm else None
    w_ref = refs.pop(0)
    r_ref = refs.pop(0) if residual else None
    extra = [refs.pop(0) for _ in range(n_extra)]
    outs = [refs.pop(0) for _ in range(n_out)]
    if norm:
        xn_ref = refs.pop(0)

        @pl.when(pl.program_id(1) == 0)
        def _():
            x = x_ref[...]
            inv = lax.rsqrt(jnp.mean(x * x, axis=-1, keepdims=True) + NORM_EPS)
            xn_ref[...] = (x * inv * g_ref[...]).astype(BF16)

        lhs = xn_ref[...]
    else:
        lhs = x_ref[...]
    y = jnp.dot(lhs, w_ref[...], preferred_element_type=F32)
    if residual:
        y = r_ref[...] + y

    def emit(lo, hi, val):
        for o_ref in outs:
            o_ref[:, lo:hi] = val.astype(o_ref.dtype)

    tn = y.shape[1]
    if epilogue is None:
        emit(0, tn, y)
    elif epilogue == "head_rms":
        gain = extra[0][...]
        for lo in range(0, tn, head_dim):
            yh = y[:, lo:lo + head_dim]
            yh = yh * lax.rsqrt(jnp.mean(yh * yh, axis=-1, keepdims=True) + NORM_EPS) * gain
            emit(lo, lo + head_dim, yh if post_scale is None else yh * post_scale)
    elif epilogue == "rotary":
        cos = extra[0][...]
        sin = extra[1][...]
        half = head_dim // 2
        for lo in range(0, tn, head_dim):
            x1 = y[:, lo:lo + half]
            x2 = y[:, lo + half:lo + head_dim]
            o1 = x1 * cos - x2 * sin
            o2 = x1 * sin + x2 * cos
            emit(lo, lo + half, o1 if post_scale is None else o1 * post_scale)
            emit(lo + half, lo + head_dim, o2 if post_scale is None else o2 * post_scale)
    elif epilogue == "log_sigmoid":
        emit(0, tn, jax.nn.log_sigmoid(y + extra[0][...]))
    else:
        raise ValueError(epilogue)


def _proj(x, w, gain=None, residual=None, epilogue=None, extra=(), out_dtypes=(F32,), head_dim=None, post_scale=None):
    m, k = x.shape
    n = w.shape[1]
    tm = _row_tile(m, PROJ_ROW_TILE)
    tn = _col_tile(n)
    norm = gain is not None
    in_specs = [pl.BlockSpec((tm, k), lambda i, j: (i, 0))]
    args = [x]
    if norm:
        in_specs.append(pl.BlockSpec((1, k), lambda i, j: (0, 0)))
        args.append(gain.reshape(1, k))
    in_specs.append(pl.BlockSpec((k, tn), lambda i, j: (0, j)))
    args.append(w)
    if residual is not None:
        in_specs.append(pl.BlockSpec((tm, tn), lambda i, j: (i, j)))
        args.append(residual)
    if epilogue == "head_rms":
        in_specs.append(pl.BlockSpec((1, head_dim), lambda i, j: (0, 0)))
        args.append(extra[0].reshape(1, head_dim))
    elif epilogue == "rotary":
        for e in extra:
            in_specs.append(pl.BlockSpec((tm, head_dim // 2), lambda i, j: (i, 0)))
            args.append(e)
    elif epilogue == "log_sigmoid":
        in_specs.append(pl.BlockSpec((1, tn), lambda i, j: (0, j)))
        args.append(extra[0].reshape(1, n))
    outs = pl.pallas_call(
        functools.partial(_proj_kernel, norm=norm, residual=residual is not None, epilogue=epilogue,
                          n_extra=len(extra), n_out=len(out_dtypes), head_dim=head_dim, post_scale=post_scale),
        grid=(m // tm, n // tn),
        in_specs=in_specs,
        out_specs=[pl.BlockSpec((tm, tn), lambda i, j: (i, j)) for _ in out_dtypes],
        out_shape=[jax.ShapeDtypeStruct((m, n), dt) for dt in out_dtypes],
        scratch_shapes=[pltpu.VMEM((tm, k), BF16)] if norm else [],
        compiler_params=pltpu.CompilerParams(
            dimension_semantics=("parallel", "arbitrary"), vmem_limit_bytes=VMEM_LIMIT_BYTES),
        name="proj" if epilogue is None else "proj_" + epilogue,
    )(*args)
    return outs[0] if len(outs) == 1 else outs


def _pad_cols(w):
    return jnp.pad(w, ((0, 0),) * (w.ndim - 1) + ((0, LANES - w.shape[-1]),))


def _fox_flash_kernel(q_ref, k_ref, v_ref, ck_ref, o_ref, *, tile, heads):
    qi = pl.program_id(2)
    dh = q_ref.shape[1] // heads
    cols = [slice(h * dh, (h + 1) * dh) for h in range(heads)]
    qs = [q_ref[:, c] for c in cols]

    def step(j, carry, diagonal):
        ks = pl.multiple_of(j * tile, tile)
        ss = []
        for h, c in enumerate(cols):
            ck = ck_ref[h:h + 1, pl.ds(ks, tile)] * LOG2E
            s = lax.dot_general(qs[h], k_ref[pl.ds(ks, tile), c], (((1,), (1,)), ((), ())),
                                preferred_element_type=F32) - ck
            if diagonal:
                row = lax.broadcasted_iota(jnp.int32, s.shape, 0)
                col = lax.broadcasted_iota(jnp.int32, s.shape, 1)
                s = jnp.where(col <= row, s, -jnp.inf)
            ss.append(s)
        m_new = [jnp.maximum(m, jnp.max(s, axis=1, keepdims=True)) for (m, _, _), s in zip(carry, ss)]
        ps = [jnp.exp2(s - m) for s, m in zip(ss, m_new)]
        out = []
        for h, c in enumerate(cols):
            m, l, acc = carry[h]
            alpha = jnp.exp2(m - m_new[h])
            l = alpha * l + jnp.sum(ps[h], axis=1, keepdims=True)
            acc = alpha * acc + jnp.dot(ps[h].astype(BF16), v_ref[pl.ds(ks, tile), c], preferred_element_type=F32)
            out.append((m_new[h], l, acc))
        return tuple(out)

    init = tuple((jnp.full((tile, 1), -jnp.inf, F32), jnp.zeros((tile, 1), F32), jnp.zeros((tile, dh), F32))
                 for _ in cols)
    carry = lax.fori_loop(0, qi, functools.partial(step, diagonal=False), init)
    carry = step(qi, carry, True)
    for c, (m, l, acc) in zip(cols, carry):
        o_ref[:, c] = (acc / l).astype(o_ref.dtype)


def _fox_flash(q, k, v, c):
    b, s, d = q.shape
    h = c.shape[2]
    dh = d // h
    hp = FOX_HEADS_PER_STEP
    tile = min(FOX_ATTN_TILE, s)
    ct = jnp.swapaxes(c, 1, 2).reshape(b, h // hp, hp, s)
    return pl.pallas_call(
        functools.partial(_fox_flash_kernel, tile=tile, heads=hp),
        grid=(b, h // hp, s // tile),
        in_specs=[
            pl.BlockSpec((None, tile, hp * dh), lambda bi, hi, qi: (bi, qi, hi)),
            pl.BlockSpec((None, s, hp * dh), lambda bi, hi, qi: (bi, 0, hi)),
            pl.BlockSpec((None, s, hp * dh), lambda bi, hi, qi: (bi, 0, hi)),
            pl.BlockSpec((None, None, hp, s), lambda bi, hi, qi: (bi, hi, 0, 0)),
        ],
        out_specs=pl.BlockSpec((None, tile, hp * dh), lambda bi, hi, qi: (bi, qi, hi)),
        out_shape=jax.ShapeDtypeStruct((b, s, d), BF16),
        compiler_params=pltpu.CompilerParams(
            dimension_semantics=("parallel", "parallel", "arbitrary"), vmem_limit_bytes=VMEM_LIMIT_BYTES),
        name="fox_flash",
    )(q, k, v, ct)


def _fox_decode_kernel(pt_ref, q_ref, kn_ref, vn_ref, c0_ref, *refs, pages, heads):
    k_refs = refs[:pages]
    v_refs = refs[pages:2 * pages]
    lf_refs = refs[2 * pages:3 * pages]
    o_ref = refs[3 * pages]
    m_ref, l_ref, acc_ref, carry_ref = refs[3 * pages + 1:]
    p = pl.program_id(1)
    rows = k_refs[0].shape[0]
    width = lf_refs[0].shape[1]

    @pl.when(p == 0)
    def _():
        m_ref[...] = jnp.full_like(m_ref, -jnp.inf)
        l_ref[...] = jnp.zeros_like(l_ref)
        acc_ref[...] = jnp.zeros_like(acc_ref)
        carry_ref[...] = c0_ref[...]

    q = q_ref[...]
    head_of_row = lax.broadcasted_iota(jnp.int32, (heads, rows), 0)
    head_of_col = lax.broadcasted_iota(jnp.int32, (heads, rows), 1) % heads
    own = head_of_row == head_of_col
    lane = lax.broadcasted_iota(jnp.int32, (SUBLANES, width), 1)
    r0 = lax.broadcasted_iota(jnp.int32, (SUBLANES, SUBLANES), 0)
    r1 = lax.broadcasted_iota(jnp.int32, (SUBLANES, SUBLANES), 1)
    later_rows = (r1 > r0).astype(F32)
    all_rows = jnp.ones((SUBLANES, SUBLANES), F32)
    carry = carry_ref[...]
    scores = []
    for g in reversed(range(pages)):
        lf = lf_refs[g][...]
        scan = lf
        total = lf
        d = heads
        while d < width:
            scan = scan + jnp.where(lane + d < width, pltpu.roll(scan, width - d, axis=1), 0.0)
            total = total + pltpu.roll(total, width - d, axis=1)
            d *= 2
        bias = scan - lf + _dot_f32(later_rows, total) + carry
        carry = carry + _dot_f32(all_rows, total)
        bias = jnp.concatenate(
            [jnp.broadcast_to(bias[r:r + 1, :], (heads, width)) for r in range(SUBLANES)], axis=1)
        s = lax.dot_general(q, k_refs[g][...], (((1,), (1,)), ((), ())), preferred_element_type=F32)
        scores.append((g, jnp.where(own, s + bias, -jnp.inf)))
    carry_ref[...] = carry
    m = m_ref[...]
    m_new = m
    for _, s in scores:
        m_new = jnp.maximum(m_new, jnp.max(s, axis=1, keepdims=True))
    alpha = jnp.exp(m - m_new)
    l = alpha * l_ref[...]
    acc = alpha * acc_ref[...]
    for g, s in scores:
        pr = jnp.exp(s - m_new)
        l = l + jnp.sum(pr, axis=1, keepdims=True)
        acc = acc + jnp.dot(pr, v_refs[g][...], preferred_element_type=F32)
    m_ref[...] = m_new
    l_ref[...] = l
    acc_ref[...] = acc

    @pl.when(p == pl.num_programs(1) - 1)
    def _():
        s_new = jnp.sum(_bf(q) * _bf(kn_ref[...]), axis=1, keepdims=True)
        m = m_ref[...]
        m_new = jnp.maximum(m, s_new)
        alpha = jnp.exp(m - m_new)
        p_new = jnp.exp(s_new - m_new)
        l = alpha * l_ref[...] + p_new
        acc = alpha * acc_ref[...] + _bf(p_new) * _bf(vn_ref[...])
        o_ref[...] = acc / l


def _fox_decode(q, k_new, v_new, log_f_new, cache_k, cache_v, cache_logf, page_table):
    b, h, dh = q.shape
    n_pool, ps = cache_k.shape[:2]
    n_pages = page_table.shape[1]
    pages = FOX_PAGES_PER_STEP
    n_steps = n_pages // pages
    rows = ps * h
    width = rows // SUBLANES
    ck = cache_k.reshape(n_pool, rows, dh)
    cv = cache_v.reshape(n_pool, rows, dh)
    clf = cache_logf.reshape(n_pool, SUBLANES, width)
    c0 = jnp.broadcast_to(jnp.tile(log_f_new, (1, width // h))[:, None, :], (b, SUBLANES, width))

    def page_map(g):
        return lambda bi, p, pt: (pt[bi, (n_steps - 1 - p) * pages + g], 0, 0)

    row_spec = pl.BlockSpec((None, h, dh), lambda bi, p, pt: (bi, 0, 0))
    grid_spec = pltpu.PrefetchScalarGridSpec(
        num_scalar_prefetch=1,
        grid=(b, n_steps),
        in_specs=[row_spec, row_spec, row_spec,
                  pl.BlockSpec((None, SUBLANES, width), lambda bi, p, pt: (bi, 0, 0))]
        + [pl.BlockSpec((None, rows, dh), page_map(g)) for g in range(pages)]
        + [pl.BlockSpec((None, rows, dh), page_map(g)) for g in range(pages)]
        + [pl.BlockSpec((None, SUBLANES, width), page_map(g)) for g in range(pages)],
        out_specs=row_spec,
        scratch_shapes=[pltpu.VMEM((h, 1), F32), pltpu.VMEM((h, 1), F32), pltpu.VMEM((h, dh), F32),
                        pltpu.VMEM((SUBLANES, width), F32)],
    )
    return pl.pallas_call(
        functools.partial(_fox_decode_kernel, pages=pages, heads=h),
        grid_spec=grid_spec,
        out_shape=jax.ShapeDtypeStruct((b, h, dh), F32),
        compiler_params=pltpu.CompilerParams(
            dimension_semantics=("parallel", "arbitrary"), vmem_limit_bytes=VMEM_LIMIT_BYTES),
        name="fox_decode",
    )(page_table, q, k_new, v_new, c0, *([ck] * pages), *([cv] * pages), *([clf] * pages))


def _conv_silu_kernel(cur_ref, prev_ref, st_ref, w_ref, o_ref, *, tiles_per_seq, head_dim, post_scale):
    taps = w_ref.shape[0]
    cur = cur_ref[...]
    first = pl.program_id(0) % tiles_per_seq == 0
    prev = jnp.where(first, st_ref[...], prev_ref[...])
    w = w_ref[...]

    def finish(conv):
        y = conv * jax.nn.sigmoid(conv)
        if head_dim is None:
            return y
        parts = []
        for lo in range(0, y.shape[1], head_dim):
            yh = y[:, lo:lo + head_dim]
            yh = yh * lax.rsqrt(jnp.sum(yh * yh, axis=-1, keepdims=True) + NORM_EPS)
            parts.append(yh if post_scale is None else yh * post_scale)
        return jnp.concatenate(parts, axis=1)

    conv = cur * w[taps - 1:taps, :]
    for s in range(1, taps):
        conv = conv + pltpu.roll(cur, s, axis=0) * w[taps - 1 - s:taps - s, :]
    o_ref[...] = finish(conv)
    top = cur[0:SUBLANES, :]
    row = lax.broadcasted_iota(jnp.int32, top.shape, 0)
    conv = top * w[taps - 1:taps, :]
    for s in range(1, taps):
        shifted = jnp.where(row < s, pltpu.roll(prev, s, axis=0), pltpu.roll(top, s, axis=0))
        conv = conv + shifted * w[taps - 1 - s:taps - s, :]
    o_ref[0:SUBLANES, :] = finish(conv)


def _conv_silu(x, state, w, col0, ncols, seq_len, head_dim=None, post_scale=None):
    m, c = x.shape
    b = m // seq_len
    taps = w.shape[0]
    tm = min(ROW_TILE, seq_len)
    tn = min(COL_TILE, ncols)
    tiles_per_seq = seq_len // tm
    j0 = col0 // tn
    st = jnp.pad(state, ((0, 0), (SUBLANES - (taps - 1), 0), (0, 0)))
    return pl.pallas_call(
        functools.partial(_conv_silu_kernel, tiles_per_seq=tiles_per_seq, head_dim=head_dim, post_scale=post_scale),
        grid=(m // tm, ncols // tn),
        in_specs=[
            pl.BlockSpec((tm, tn), lambda i, j: (i, j + j0)),
            pl.BlockSpec((SUBLANES, tn), lambda i, j: (jnp.maximum(i * (tm // SUBLANES) - 1, 0), j + j0)),
            pl.BlockSpec((None, SUBLANES, tn), lambda i, j: (i // tiles_per_seq, 0, j + j0)),
            pl.BlockSpec((taps, tn), lambda i, j: (0, j + j0)),
        ],
        out_specs=pl.BlockSpec((tm, tn), lambda i, j: (i, j)),
        out_shape=jax.ShapeDtypeStruct((m, ncols), F32),
        compiler_params=pltpu.CompilerParams(
            dimension_semantics=("parallel", "parallel"), vmem_limit_bytes=VMEM_LIMIT_BYTES),
        name="conv_silu",
    )(x, x, st, w)


def _gdn_scan_kernel(q_ref, k_ref, v_ref, bcol_ref, gcol_ref, grow_ref, z_ref, gn_ref, s0_ref, o_ref, s_ref, *,
                     rep):
    c = q_ref.shape[0]
    dk = s_ref.shape[1]
    dv = s_ref.shape[2]
    hg = s_ref.shape[0]

    @pl.when(pl.program_id(2) == 0)
    def _():
        s_ref[...] = s0_ref[...]

    row = lax.broadcasted_iota(jnp.int32, (c, c), 0)
    col = lax.broadcasted_iota(jnp.int32, (c, c), 1)
    tril = row >= col
    strict = row > col
    g_cols = _dot_f32(tril.astype(F32), gcol_ref[...])
    g_rows = _dot_f32(grow_ref[...], (row <= col).astype(F32))
    heads = []
    for h in range(hg):
        j = h // rep
        qh = q_ref[:, j * dk:(j + 1) * dk]
        kh = k_ref[:, j * dk:(j + 1) * dk]
        gc = g_cols[:, h:h + 1]
        gr = g_rows[h:h + 1, :]
        decay = jnp.where(tril, jnp.exp(jnp.where(tril, gc - gr, 0.0)), 0.0)
        beta = bcol_ref[:, h:h + 1]
        kb = kh * beta
        exp_g = jnp.exp(gc)
        low = jnp.where(strict, _dot_nt(kb, kh) * decay, 0.0)
        attn = jnp.where(tril, _dot_nt(qh, kh) * decay, 0.0)
        x = jnp.concatenate([v_ref[:, h * dv:(h + 1) * dv] * beta, kb * exp_g], axis=1)
        heads.append(dict(qh=qh, kh=kh, gc=gc, exp_g=exp_g, low=low, attn=attn, x=x))
    for hd in heads:
        hd["x"] = hd["x"] - _dot(hd["low"], hd["x"])
        hd["p"] = hd["low"]
    n = 2
    while n < c:
        for hd in heads:
            hd["p"] = _dot(hd["p"], hd["p"])
        for hd in heads:
            hd["x"] = hd["x"] + _dot(hd["p"], hd["x"])
        n *= 2
    for h, hd in enumerate(heads):
        hd["s"] = s_ref[h]
        hd["v_new"] = hd["x"][:, :dv] - _dot(hd["x"][:, dv:], hd["s"])
    for h, hd in enumerate(heads):
        o = _dot(hd["qh"] * hd["exp_g"], hd["s"]) + _dot(hd["attn"], hd["v_new"])
        o_ref[:, h * dv:(h + 1) * dv] = _gated_rms(o, gn_ref[...], z_ref[:, h * dv:(h + 1) * dv]).astype(o_ref.dtype)
    for h, hd in enumerate(heads):
        gc = hd["gc"]
        g_last = gc[c - 1:c, :]
        s_ref[h] = hd["s"] * jnp.exp(g_last) + _dot_tn(hd["kh"] * jnp.exp(g_last - gc), hd["v_new"])


def _gdn_scan(q, k, v, beta, g, z, o_norm, s0):
    b, t, _ = q.shape
    hv, dk, dv = s0.shape[1:]
    hq = q.shape[2] // dk
    rep = hv // hq
    c = GDN_CHUNK
    nc = t // c
    hg = min(GDN_HEAD_GROUP, hv)
    ng = hv // hg
    cols = lambda a: a.reshape(b, t, ng, hg).transpose(0, 2, 1, 3)
    g_rows = g.reshape(b, nc, c, hv).transpose(0, 1, 3, 2)
    o, s = pl.pallas_call(
        functools.partial(_gdn_scan_kernel, rep=rep),
        grid=(b, ng, nc),
        in_specs=[
            pl.BlockSpec((None, c, hg // rep * dk), lambda bi, gi, ci: (bi, ci, gi)),
            pl.BlockSpec((None, c, hg // rep * dk), lambda bi, gi, ci: (bi, ci, gi)),
            pl.BlockSpec((None, c, hg * dv), lambda bi, gi, ci: (bi, ci, gi)),
            pl.BlockSpec((None, None, c, hg), lambda bi, gi, ci: (bi, gi, ci, 0)),
            pl.BlockSpec((None, None, c, hg), lambda bi, gi, ci: (bi, gi, ci, 0)),
            pl.BlockSpec((None, None, hg, c), lambda bi, gi, ci: (bi, ci, gi, 0)),
            pl.BlockSpec((None, c, hg * dv), lambda bi, gi, ci: (bi, ci, gi)),
            pl.BlockSpec((1, dv), lambda bi, gi, ci: (0, 0)),
            pl.BlockSpec((None, hg, dk, dv), lambda bi, gi, ci: (bi, gi, 0, 0)),
        ],
        out_specs=[
            pl.BlockSpec((None, c, hg * dv), lambda bi, gi, ci: (bi, ci, gi)),
            pl.BlockSpec((None, hg, dk, dv), lambda bi, gi, ci: (bi, gi, 0, 0)),
        ],
        out_shape=[jax.ShapeDtypeStruct((b, t, hv * dv), BF16), jax.ShapeDtypeStruct((b, hv, dk, dv), F32)],
        compiler_params=pltpu.CompilerParams(
            dimension_semantics=("parallel", "parallel", "arbitrary"), vmem_limit_bytes=VMEM_LIMIT_BYTES),
        name="gdn_scan",
    )(q, k, v, cols(beta), cols(g), g_rows, z, o_norm.reshape(1, dv), s0)
    return o, s


def _ret_scan_kernel(lg_ref, q_ref, k_ref, v_ref, gate_ref, gn_ref, s0_ref, o_ref, s_ref):
    c = q_ref.shape[0]
    heads, dk, dv = s_ref.shape

    @pl.when(pl.program_id(2) == 0)
    def _():
        s_ref[...] = s0_ref[...]

    row = lax.broadcasted_iota(jnp.int32, (c, c), 0)
    col = lax.broadcasted_iota(jnp.int32, (c, c), 1)
    tril = row >= col
    dist = jnp.where(tril, row - col, 0).astype(F32)
    pos = lax.broadcasted_iota(jnp.int32, (c, 1), 0).astype(F32)
    hs = []
    for h in range(heads):
        lg = lg_ref[h]
        q = q_ref[:, h * dk:(h + 1) * dk]
        k = k_ref[:, h * dk:(h + 1) * dk]
        qk = lax.dot_general(q, k, (((1,), (1,)), ((), ())), preferred_element_type=F32)
        hs.append(dict(lg=lg, q=q, k=k, qk=qk * jnp.where(tril, jnp.exp(lg * dist), 0.0), s=s_ref[h]))
    for h, hd in enumerate(hs):
        v = v_ref[:, h * dv:(h + 1) * dv]
        inner = jnp.dot(hd["qk"], v, preferred_element_type=F32)
        cross = jnp.dot(hd["q"] * jnp.exp(hd["lg"] * (pos + 1.0)), hd["s"], preferred_element_type=F32)
        gate = gate_ref[:, h * dv:(h + 1) * dv]
        o_ref[:, h * dv:(h + 1) * dv] = _gated_group_norm(inner + cross, gn_ref[h], gate).astype(o_ref.dtype)
    for h, hd in enumerate(hs):
        lg = hd["lg"]
        kd = hd["k"] * jnp.exp(lg * (c - 1.0 - pos))
        s_ref[h] = hd["s"] * jnp.exp(lg * c) + lax.dot_general(
            kd, v_ref[:, h * dv:(h + 1) * dv], (((0,), (0,)), ((), ())), preferred_element_type=F32)


def _ret_scan(q, k, v, gate, gn_gain, s0, log_gamma):
    b, t, _ = q.shape
    h, dk, dv = s0.shape[1:]
    c = RET_CHUNK
    hp = RET_HEADS_PER_STEP
    seq = lambda width: pl.BlockSpec((None, c, hp * width), lambda bi, hi, ci: (bi, ci, hi))
    state = pl.BlockSpec((None, hp, dk, dv), lambda bi, hi, ci: (bi, hi, 0, 0))
    return pl.pallas_call(
        _ret_scan_kernel,
        grid=(b, h // hp, t // c),
        in_specs=[
            pl.BlockSpec((hp, 1, 1), lambda bi, hi, ci: (hi, 0, 0)),
            seq(dk), seq(dk), seq(dv), seq(dv),
            pl.BlockSpec((hp, 1, dv), lambda bi, hi, ci: (hi, 0, 0)),
            state,
        ],
        out_specs=[seq(dv), state],
        out_shape=[jax.ShapeDtypeStruct((b, t, h * dv), BF16), jax.ShapeDtypeStruct((b, h, dk, dv), F32)],
        compiler_params=pltpu.CompilerParams(
            dimension_semantics=("parallel", "parallel", "arbitrary"), vmem_limit_bytes=VMEM_LIMIT_BYTES),
        name="ret_scan",
    )(log_gamma.reshape(h, 1, 1), q, k, v, gate, gn_gain.reshape(h, 1, dv), s0)


def _first_row(a):
    row = lax.broadcasted_iota(jnp.int32, (SUBLANES, a.shape[1]), 0)
    return jnp.where(row == 0, jnp.broadcast_to(a, (SUBLANES, a.shape[1])), 0.0)


def _gdn_step_kernel(q_ref, k_ref, v_ref, beta_ref, g_ref, z_ref, gn_ref, s0_ref, o_ref, s_ref, *, rep):
    hv = s0_ref.shape[0]
    for h in range(hv):
        j = h // rep
        q = q_ref[j:j + 1, :]
        k = k_ref[j:j + 1, :]
        v = v_ref[h:h + 1, :]
        beta = beta_ref[h:h + 1, :]
        eg = jnp.exp(g_ref[h:h + 1, :])
        s = s0_ref[h]
        kb = k * beta
        lhs = jnp.concatenate([kb * eg, q * eg, jnp.zeros((SUBLANES - 2, q.shape[1]), F32)], axis=0)
        both = jnp.dot(lhs, s, preferred_element_type=F32)
        v_new = v * beta - both[0:1, :]
        attn = jnp.sum(_bf(q) * _bf(k), axis=1, keepdims=True)
        o = both[1:2, :] + _bf(attn) * _bf(v_new)
        o_ref[h:h + 1, :] = _gated_rms(o, gn_ref[...], z_ref[h:h + 1, :])
        s_ref[h] = s * eg + lax.dot_general(_first_row(k), _first_row(v_new), (((0,), (0,)), ((), ())),
                                            preferred_element_type=F32)


def _gdn_step(q, k, v, beta, g, z, o_norm, s0):
    b, hq, dk = q.shape
    hv, dv = v.shape[1:]
    spec = lambda *shape: pl.BlockSpec((None,) + shape, lambda bi: (bi,) + (0,) * len(shape))
    return pl.pallas_call(
        functools.partial(_gdn_step_kernel, rep=hv // hq),
        grid=(b,),
        in_specs=[spec(hq, dk), spec(hq, dk), spec(hv, dv), spec(hv, 1), spec(hv, 1), spec(hv, dv),
                  pl.BlockSpec((1, dv), lambda bi: (0, 0)), spec(hv, dk, dv)],
        out_specs=[spec(hv, dv), spec(hv, dk, dv)],
        out_shape=[jax.ShapeDtypeStruct((b, hv, dv), F32), jax.ShapeDtypeStruct((b, hv, dk, dv), F32)],
        compiler_params=pltpu.CompilerParams(dimension_semantics=("parallel",), vmem_limit_bytes=VMEM_LIMIT_BYTES),
        name="gdn_step",
    )(q, k, v, beta[..., None], g[..., None], z, o_norm.reshape(1, dv), s0)


def _ret_step_kernel(lg_ref, q_ref, k_ref, v_ref, gate_ref, gn_ref, s0_ref, o_ref, s_ref):
    gamma = jnp.exp(lg_ref[...])
    q = q_ref[...]
    k = k_ref[...]
    v = v_ref[...]
    s = s0_ref[...]
    cross = jnp.dot(_first_row(q * gamma), s, preferred_element_type=F32)[0:1, :]
    attn = jnp.sum(_bf(q) * _bf(k), axis=1, keepdims=True)
    o_ref[...] = _gated_group_norm(_bf(attn) * _bf(v) + cross, gn_ref[...], gate_ref[...])
    s_ref[...] = s * gamma + lax.dot_general(_first_row(k), _first_row(v), (((0,), (0,)), ((), ())),
                                             preferred_element_type=F32)


def _ret_step(q, k, v, gate, gn_gain, s0, log_gamma):
    b, h, dk = q.shape
    dv = v.shape[2]
    vec = lambda n: pl.BlockSpec((None, None, 1, n), lambda bi, hi: (bi, hi, 0, 0))
    mat = pl.BlockSpec((None, None, dk, dv), lambda bi, hi: (bi, hi, 0, 0))
    o, s = pl.pallas_call(
        _ret_step_kernel,
        grid=(b, h),
        in_specs=[pl.BlockSpec((None, 1, 1), lambda bi, hi: (hi, 0, 0)), vec(dk), vec(dk), vec(dv), vec(dv),
                  pl.BlockSpec((None, 1, dv), lambda bi, hi: (hi, 0, 0)), mat],
        out_specs=[vec(dv), mat],
        out_shape=[jax.ShapeDtypeStruct((b, h, 1, dv), F32), jax.ShapeDtypeStruct((b, h, dk, dv), F32)],
        compiler_params=pltpu.CompilerParams(
            dimension_semantics=("parallel", "parallel"), vmem_limit_bytes=VMEM_LIMIT_BYTES),
        name="ret_step",
    )(log_gamma.reshape(h, 1, 1), q[:, :, None, :], k[:, :, None, :], v[:, :, None, :], gate[:, :, None, :],
      gn_gain.reshape(h, 1, dv), s0)
    return o[:, :, 0, :], s


def _l2_norm(x):
    return x * lax.rsqrt(jnp.sum(x * x, axis=-1, keepdims=True) + NORM_EPS)


def _fox_mixer(x, b, t, g, w_in, b_f, q_norm, k_norm, w_o, cache):
    d = D_MODEL
    prompt = cache is None
    scale = FOX_HEAD_DIM ** -0.5
    head_rms = dict(gain=g, epilogue="head_rms", head_dim=FOX_HEAD_DIM)
    kv_dtypes = (F32, BF16) if prompt else (F32,)
    k = _proj(x, w_in[:, d:2 * d], extra=(k_norm,), out_dtypes=kv_dtypes, **head_rms)
    v = _proj(x, w_in[:, 2 * d:3 * d], gain=g, out_dtypes=kv_dtypes)
    log_f = _proj(x, _pad_cols(w_in[:, 3 * d:]), gain=g, epilogue="log_sigmoid", extra=(_pad_cols(b_f),))
    log_f = log_f[:, :FOX_HEADS].reshape(b, t, FOX_HEADS)
    if prompt:
        (k, k_bf), (v, v_bf) = k, v
        q_bf = _proj(x, w_in[:, :d], extra=(q_norm,), out_dtypes=(BF16,), post_scale=scale * LOG2E, **head_rms)
        seq = lambda a: a.reshape(b, t, d)
        o = _fox_flash(seq(q_bf), seq(k_bf), seq(v_bf), jnp.cumsum(log_f, axis=1))
    else:
        assert t == 1
        q = _proj(x, w_in[:, :d], extra=(q_norm,), post_scale=scale, **head_rms)
        heads = lambda a: a.reshape(b, FOX_HEADS, FOX_HEAD_DIM)
        o = _fox_decode(heads(q), heads(k), heads(v), log_f[:, 0], *cache).astype(BF16)
    y = _proj(o.reshape(b * t, d), w_o, residual=x)
    hd = (b, t, FOX_HEADS, FOX_HEAD_DIM)
    return y, (k.reshape(hd), v.reshape(hd), log_f)


def _gdn_mixer(x, b, t, g, w_in, conv_w, a_log, dt_bias, o_norm, w_o, conv_buf, s0):
    n_main = GDN_CONV_DIM + GDN_V_DIM
    qkv = _proj(x, w_in[:, :GDN_CONV_DIM], gain=g)
    z = _proj(x, w_in[:, GDN_CONV_DIM:n_main], gain=g)
    tail = _proj(x, _pad_cols(w_in[:, n_main:]), gain=g)
    b_in = tail[:, :GDN_V_HEADS].reshape(b, t, GDN_V_HEADS)
    a_in = tail[:, GDN_V_HEADS:2 * GDN_V_HEADS].reshape(b, t, GDN_V_HEADS)
    beta = jax.nn.sigmoid(b_in)
    gg = -jnp.exp(a_log) * jax.nn.softplus(a_in + dt_bias)
    if t % GDN_CHUNK == 0:
        conv = functools.partial(_conv_silu, qkv, conv_buf, conv_w, seq_len=t)
        q = conv(0, GDN_QK_DIM, head_dim=GDN_DK, post_scale=GDN_DK ** -0.5)
        k = conv(GDN_QK_DIM, GDN_QK_DIM, head_dim=GDN_DK)
        v = conv(2 * GDN_QK_DIM, GDN_V_DIM)
        new_buf = qkv.reshape(b, t, GDN_CONV_DIM)[:, t - (GDN_CONV_WIDTH - 1):]
        seq = lambda a: a.reshape(b, t, a.shape[1])
        o, s_new = _gdn_scan(seq(q), seq(k), seq(v), beta, gg, seq(z), o_norm, s0)
    else:
        assert t == 1
        xc = jnp.concatenate([conv_buf, qkv[:, None, :]], axis=1)
        conv = xc[:, 0] * conv_w[0]
        for w in range(1, GDN_CONV_WIDTH):
            conv = conv + xc[:, w] * conv_w[w]
        conv = jax.nn.silu(conv)
        new_buf = xc[:, 1:]
        heads = lambda a: a.reshape(b, a.shape[1] // GDN_DK, GDN_DK)
        q = _l2_norm(heads(conv[:, :GDN_QK_DIM])) * GDN_DK ** -0.5
        k = _l2_norm(heads(conv[:, GDN_QK_DIM:2 * GDN_QK_DIM]))
        o, s_new = _gdn_step(q, k, heads(conv[:, 2 * GDN_QK_DIM:]), beta[:, 0], gg[:, 0], heads(z), o_norm, s0)
    y = _proj(o.reshape(b * t, GDN_V_DIM).astype(BF16), w_o, residual=x)
    return y, (new_buf, s_new)


def _ret_mixer(x, b, t, g, w_in, gn_gain, w_o, s0, pos0):
    d = D_MODEL
    half = RET_DK // 2
    inv = ROPE_BASE ** (-jnp.arange(half, dtype=F32) / half)
    ang = (pos0 + jnp.arange(t)).astype(F32)[:, None] * inv[None, :]
    rotary = dict(gain=g, epilogue="rotary", extra=(jnp.tile(jnp.cos(ang), (b, 1)), jnp.tile(jnp.sin(ang), (b, 1))),
                  head_dim=RET_DK)
    q = _proj(x, w_in[:, :d], **rotary)
    k = _proj(x, w_in[:, d:2 * d], post_scale=RET_DK ** -0.5, **rotary)
    v = _proj(x, w_in[:, 2 * d:2 * d + RET_V_DIM], gain=g)
    gate = _proj(x, w_in[:, 2 * d + RET_V_DIM:], gain=g)
    log_gamma = jnp.log1p(-jnp.exp2(-5.0 - jnp.arange(RET_HEADS, dtype=F32)))
    if t % RET_CHUNK == 0:
        seq = lambda a: a.reshape(b, t, a.shape[1])
        o, s_new = _ret_scan(seq(q), seq(k), seq(v), seq(gate), gn_gain, s0, log_gamma)
    else:
        assert t == 1
        heads = lambda a: a.reshape(b, RET_HEADS, a.shape[1] // RET_HEADS)
        o, s_new = _ret_step(heads(q), heads(k), heads(v), heads(gate), gn_gain, s0, log_gamma)
    y = _proj(o.reshape(b * t, RET_V_DIM).astype(BF16), w_o, residual=x)
    return y, (s_new,)


def kernel(x_prompt, x_sample, cache_k_l0, cache_v_l0, cache_logf_l0, state_conv_l1, state_gdn_l1, state_ret_l2, cache_k_l3, cache_v_l3, cache_logf_l3, page_table, norm_g, ffn_w_in, ffn_w_out, fox_w_in_l0, fox_b_f_l0, fox_q_norm_l0, fox_k_norm_l0, fox_w_o_l0, gdn_w_in_l1, gdn_conv_w_l1, gdn_a_log_l1, gdn_dt_bias_l1, gdn_o_norm_l1, gdn_w_o_l1, ret_w_in_l2, ret_gn_l2, ret_w_o_l2, fox_w_in_l3, fox_b_f_l3, fox_q_norm_l3, fox_k_norm_l3, fox_w_o_l3):
    ffn_bf16 = {}
    fox0 = (fox_w_in_l0.astype(BF16), fox_b_f_l0, fox_q_norm_l0, fox_k_norm_l0, fox_w_o_l0.astype(BF16))
    fox3 = (fox_w_in_l3.astype(BF16), fox_b_f_l3, fox_q_norm_l3, fox_k_norm_l3, fox_w_o_l3.astype(BF16))
    gdn = (gdn_w_in_l1.astype(BF16), gdn_conv_w_l1, gdn_a_log_l1, gdn_dt_bias_l1, gdn_o_norm_l1,
           gdn_w_o_l1.astype(BF16))
    ret = (ret_w_in_l2.astype(BF16), ret_gn_l2, ret_w_o_l2.astype(BF16))

    def ffn(x, sample, i, which):
        g = norm_g[i, 2 * which]
        if sample:
            x, ffn_bf16[i, which] = _ffn_casting(x, g, ffn_w_in, ffn_w_out, i, which)
            return x
        return _ffn(x, g, ffn_bf16[i, which])

    def run(x3, sample):
        b, t, _ = x3.shape
        x = x3.reshape(b * t, D_MODEL)
        states = []
        for i in range(DEPTH):
            x = ffn(x, sample, i, 0)
            g = norm_g[i, 1]
            if i == 0:
                cache = (cache_k_l0, cache_v_l0, cache_logf_l0, page_table) if sample else None
                x, st = _fox_mixer(x, b, t, g, *fox0, cache)
            elif i == 1:
                if sample:
                    conv0, s0 = state_conv_l1, state_gdn_l1
                else:
                    conv0 = jnp.zeros((b, GDN_CONV_WIDTH - 1, GDN_CONV_DIM), F32)
                    s0 = jnp.zeros((b, GDN_V_HEADS, GDN_DK, GDN_DV), F32)
                x, st = _gdn_mixer(x, b, t, g, *gdn, conv0, s0)
            elif i == 2:
                s0 = state_ret_l2 if sample else jnp.zeros((b, RET_HEADS, RET_DK, RET_DV), F32)
                x, st = _ret_mixer(x, b, t, g, *ret, s0, PAST_LEN if sample else 0)
            else:
                cache = (cache_k_l3, cache_v_l3, cache_logf_l3, page_table) if sample else None
                x, st = _fox_mixer(x, b, t, g, *fox3, cache)
            x = ffn(x, sample, i, 1)
            states.append(st)
        return x.reshape(b, t, D_MODEL), states

    ys, st_s = run(x_sample, True)
    yp, st_p = run(x_prompt, False)
    new_state = []
    for sp, ss in zip(st_p, st_s):
        new_state += [*sp, *ss]
    return (yp, ys, *new_state)
```

```python
import functools

import jax
import jax.numpy as jnp
from jax import lax
from jax.experimental import pallas as pl
from jax.experimental.pallas import tpu as pltpu

F32 = jnp.float32
BF16 = jnp.bfloat16

D_MODEL = 2048
DEPTH = 4
PAST_LEN = 16384
D_FF = 5632
NORM_EPS = 1e-6
GN_EPS = 1e-5
FFN_HALF = 0.5

FOX_HEADS = 16
FOX_HEAD_DIM = D_MODEL // FOX_HEADS

GDN_QK_HEADS = 16
GDN_V_HEADS = 32
GDN_DK = 128
GDN_DV = 128
GDN_QK_DIM = GDN_QK_HEADS * GDN_DK
GDN_V_DIM = GDN_V_HEADS * GDN_DV
GDN_CONV_DIM = 2 * GDN_QK_DIM + GDN_V_DIM
GDN_CONV_WIDTH = 4
GDN_CHUNK = 64

RET_HEADS = 8
RET_DK = D_MODEL // RET_HEADS
RET_DV = 2 * RET_DK
RET_V_DIM = RET_HEADS * RET_DV
RET_CHUNK = 128
ROPE_BASE = 10000.0
LOG2E = 1.4426950408889634

LANES = 128
SUBLANES = 8
VMEM_LIMIT_BYTES = 56 * 1024 * 1024
ROW_TILE = 512
PROJ_ROW_TILE = 1024
COL_TILE = 1024
FF_TILE = 512
FOX_ATTN_TILE = 512
FOX_HEADS_PER_STEP = 2
FOX_PAGES_PER_STEP = 8
GDN_HEAD_GROUP = 32
RET_HEADS_PER_STEP = 4


def _row_tile(m, tile=ROW_TILE):
    return tile if m % tile == 0 else m


def _col_tile(n):
    for t in (COL_TILE, 768, 512, 256, 128):
        if n % t == 0:
            return t
    return n


def _dot(a, b):
    return jnp.dot(a.astype(BF16), b.astype(BF16), preferred_element_type=F32)


def _dot_nt(a, b):
    return lax.dot_general(a.astype(BF16), b.astype(BF16), (((1,), (1,)), ((), ())), preferred_element_type=F32)


def _dot_tn(a, b):
    return lax.dot_general(a.astype(BF16), b.astype(BF16), (((0,), (0,)), ((), ())), preferred_element_type=F32)


def _dot_f32(a, b):
    return jnp.dot(a, b, preferred_element_type=F32, precision=lax.Precision.HIGHEST)


def _bf(a):
    return a.astype(BF16).astype(F32)


def _gated_rms(o, gain, z):
    o = o * lax.rsqrt(jnp.mean(o * o, axis=-1, keepdims=True) + NORM_EPS) * gain
    return o * (z * jax.nn.sigmoid(z))


def _gated_group_norm(o, gain, gate):
    mu = jnp.mean(o, axis=-1, keepdims=True)
    var = jnp.mean(jnp.square(o - mu), axis=-1, keepdims=True)
    o = (o - mu) * lax.rsqrt(var + GN_EPS) * gain
    return (gate * jax.nn.sigmoid(gate)) * o


def _ffn_kernel(x_ref, g_ref, wg_ref, wu_ref, wo_ref, o_ref, *rest):
    *w_copies, xn_ref = rest
    j = pl.program_id(1)

    @pl.when(j == 0)
    def _():
        x = x_ref[...]
        inv = lax.rsqrt(jnp.mean(x * x, axis=-1, keepdims=True) + NORM_EPS)
        xn_ref[...] = (x * inv * g_ref[...]).astype(BF16)
        o_ref[...] = jnp.zeros_like(o_ref)

    wg, wu, wo = (r[...].astype(BF16) for r in (wg_ref, wu_ref, wo_ref))
    for dst, w in zip(w_copies, (wg, wu, wo)):
        dst[...] = w
    xn = xn_ref[...]
    gate = jnp.dot(xn, wg, preferred_element_type=F32)
    up = jnp.dot(xn, wu, preferred_element_type=F32)
    act = (gate * jax.nn.sigmoid(gate) * up).astype(BF16)
    o_ref[...] += jnp.dot(act, wo, preferred_element_type=F32)

    @pl.when(j == pl.num_programs(1) - 1)
    def _():
        o_ref[...] = x_ref[...] + FFN_HALF * o_ref[...]


def _ffn_weight_specs(d):
    tf = FF_TILE
    return [pl.BlockSpec((None, d, tf), lambda i, j: (j, 0, 0)), pl.BlockSpec((None, d, tf), lambda i, j: (j, 0, 0)),
            pl.BlockSpec((tf, d), lambda i, j: (j, 0))]


def _ffn_pallas(x, g, weights, w_specs, w_shapes, copies):
    m, d = x.shape
    tm = _row_tile(m)
    return pl.pallas_call(
        _ffn_kernel,
        grid=(m // tm, w_shapes[2][0] // FF_TILE),
        in_specs=[pl.BlockSpec((tm, d), lambda i, j: (i, 0)), pl.BlockSpec((1, d), lambda i, j: (0, 0))] + w_specs,
        out_specs=[pl.BlockSpec((tm, d), lambda i, j: (i, 0))] + (_ffn_weight_specs(d) if copies else []),
        out_shape=[jax.ShapeDtypeStruct((m, d), F32)]
        + ([jax.ShapeDtypeStruct(s, BF16) for s in w_shapes] if copies else []),
        scratch_shapes=[pltpu.VMEM((tm, d), BF16)],
        compiler_params=pltpu.CompilerParams(
            dimension_semantics=("parallel", "arbitrary"), vmem_limit_bytes=VMEM_LIMIT_BYTES),
        name="ffn",
    )(x, g.reshape(1, d), *weights)


def _ffn(x, g, weights):
    return _ffn_pallas(x, g, weights, _ffn_weight_specs(x.shape[1]), [w.shape for w in weights], copies=False)[0]


def _ffn_casting(x, g, w_in, w_out, layer, which):
    d, f = w_out.shape[3], w_out.shape[2]
    tf = FF_TILE
    nf = f // tf
    specs = [pl.BlockSpec((None, None, d, tf), lambda i, j: (layer, which, 0, j)),
             pl.BlockSpec((None, None, d, tf), lambda i, j: (layer, which, 0, j + nf)),
             pl.BlockSpec((None, None, tf, d), lambda i, j: (layer, which, j, 0))]
    y, *w_bf = _ffn_pallas(x, g, (w_in, w_in, w_out), specs, [(nf, d, tf), (nf, d, tf), (f, d)], copies=True)
    return y, tuple(w_bf)


def _proj_kernel(*refs, norm, residual, epilogue, n_extra, n_out, head_dim, post_scale, emit_xn):
    refs = list(refs)
    x_ref = refs.pop(0)
    g_ref = refs.pop(0) if norm else None
    w_ref = refs.pop(0)
    r_ref = refs.pop(0) if residual else None
    extra = [refs.pop(0) for _ in range(n_extra)]
    outs = [refs.pop(0) for _ in range(n_out)]
    if norm:
        xn_out_ref = refs.pop(0) if emit_xn else None
        xn_ref = refs.pop(0)

        @pl.when(pl.program_id(1) == 0)
        def _():
            x = x_ref[...]
            inv = lax.rsqrt(jnp.mean(x * x, axis=-1, keepdims=True) + NORM_EPS)
            xn = (x * inv * g_ref[...]).astype(BF16)
            xn_ref[...] = xn
            if emit_xn:
                xn_out_ref[...] = xn

        lhs = xn_ref[...]
    else:
        lhs = x_ref[...]
    y = jnp.dot(lhs, w_ref[...], preferred_element_type=F32)
    if residual:
        y = r_ref[...] + y

    def emit(lo, hi, val):
        for o_ref in outs:
            o_ref[:, lo:hi] = val.astype(o_ref.dtype)

    tn = y.shape[1]
    if epilogue is None:
        emit(0, tn, y)
    elif epilogue == "head_rms":
        gain = extra[0][...]
        for lo in range(0, tn, head_dim):
            yh = y[:, lo:lo + head_dim]
            yh = yh * lax.rsqrt(jnp.mean(yh * yh, axis=-1, keepdims=True) + NORM_EPS) * gain
            emit(lo, lo + head_dim, yh if post_scale is None else yh * post_scale)
    elif epilogue == "rotary":
        cos = extra[0][...]
        sin = extra[1][...]
        half = head_dim // 2
        for lo in range(0, tn, head_dim):
            x1 = y[:, lo:lo + half]
            x2 = y[:, lo + half:lo + head_dim]
            o1 = x1 * cos - x2 * sin
            o2 = x1 * sin + x2 * cos
            emit(lo, lo + half, o1 if post_scale is None else o1 * post_scale)
            emit(lo + half, lo + head_dim, o2 if post_scale is None else o2 * post_scale)
    elif epilogue == "log_sigmoid":
        emit(0, tn, jax.nn.log_sigmoid(y + extra[0][...]))
    else:
        raise ValueError(epilogue)


def _proj(x, w, gain=None, residual=None, epilogue=None, extra=(), out_dtypes=(F32,), head_dim=None, post_scale=None,
          emit_xn=False):
    m, k = x.shape
    n = w.shape[1]
    tm = _row_tile(m, PROJ_ROW_TILE)
    tn = _col_tile(n)
    norm = gain is not None
    in_specs = [pl.BlockSpec((tm, k), lambda i, j: (i, 0))]
    args = [x]
    if norm:
        in_specs.append(pl.BlockSpec((1, k), lambda i, j: (0, 0)))
        args.append(gain.reshape(1, k))
    in_specs.append(pl.BlockSpec((k, tn), lambda i, j: (0, j)))
    args.append(w)
    if residual is not None:
        in_specs.append(pl.BlockSpec((tm, tn), lambda i, j: (i, j)))
        args.append(residual)
    if epilogue == "head_rms":
        in_specs.append(pl.BlockSpec((1, head_dim), lambda i, j: (0, 0)))
        args.append(extra[0].reshape(1, head_dim))
    elif epilogue == "rotary":
        for e in extra:
            in_specs.append(pl.BlockSpec((tm, head_dim // 2), lambda i, j: (i, 0)))
            args.append(e)
    elif epilogue == "log_sigmoid":
        in_specs.append(pl.BlockSpec((1, tn), lambda i, j: (0, j)))
        args.append(extra[0].reshape(1, n))
    emit_xn = emit_xn and norm
    outs = pl.pallas_call(
        functools.partial(_proj_kernel, norm=norm, residual=residual is not None, epilogue=epilogue,
                          n_extra=len(extra), n_out=len(out_dtypes), head_dim=head_dim, post_scale=post_scale,
                          emit_xn=emit_xn),
        grid=(m // tm, n // tn),
        in_specs=in_specs,
        out_specs=[pl.BlockSpec((tm, tn), lambda i, j: (i, j)) for _ in out_dtypes]
        + ([pl.BlockSpec((tm, k), lambda i, j: (i, 0))] if emit_xn else []),
        out_shape=[jax.ShapeDtypeStruct((m, n), dt) for dt in out_dtypes]
        + ([jax.ShapeDtypeStruct((m, k), BF16)] if emit_xn else []),
        scratch_shapes=[pltpu.VMEM((tm, k), BF16)] if norm else [],
        compiler_params=pltpu.CompilerParams(
            dimension_semantics=("parallel", "arbitrary"), vmem_limit_bytes=VMEM_LIMIT_BYTES),
        name="proj" if epilogue is None else "proj_" + epilogue,
    )(*args)
    return outs[0] if len(outs) == 1 else outs


def _pad_cols(w):
    return jnp.pad(w, ((0, 0),) * (w.ndim - 1) + ((0, LANES - w.shape[-1]),))


def _fox_flash_kernel(q_ref, k_ref, v_ref, ck_ref, o_ref, *, tile, heads):
    qi = pl.program_id(2)
    dh = q_ref.shape[1] // heads
    cols = [slice(h * dh, (h + 1) * dh) for h in range(heads)]
    qs = [q_ref[:, c] for c in cols]

    def step(j, carry, diagonal):
        ks = pl.multiple_of(j * tile, tile)
        ss = []
        for h, c in enumerate(cols):
            ck = ck_ref[h:h + 1, pl.ds(ks, tile)] * LOG2E
            s = lax.dot_general(qs[h], k_ref[pl.ds(ks, tile), c], (((1,), (1,)), ((), ())),
                                preferred_element_type=F32) - ck
            if diagonal:
                row = lax.broadcasted_iota(jnp.int32, s.shape, 0)
                col = lax.broadcasted_iota(jnp.int32, s.shape, 1)
                s = jnp.where(col <= row, s, -jnp.inf)
            ss.append(s)
        m_new = [jnp.maximum(m, jnp.max(s, axis=1, keepdims=True)) for (m, _, _), s in zip(carry, ss)]
        ps = [jnp.exp2(s - m) for s, m in zip(ss, m_new)]
        out = []
        for h, c in enumerate(cols):
            m, l, acc = carry[h]
            alpha = jnp.exp2(m - m_new[h])
            l = alpha * l + jnp.sum(ps[h], axis=1, keepdims=True)
            acc = alpha * acc + jnp.dot(ps[h].astype(BF16), v_ref[pl.ds(ks, tile), c], preferred_element_type=F32)
            out.append((m_new[h], l, acc))
        return tuple(out)

    init = tuple((jnp.full((tile, 1), -jnp.inf, F32), jnp.zeros((tile, 1), F32), jnp.zeros((tile, dh), F32))
                 for _ in cols)
    carry = lax.fori_loop(0, qi, functools.partial(step, diagonal=False), init)
    carry = step(qi, carry, True)
    for c, (m, l, acc) in zip(cols, carry):
        o_ref[:, c] = (acc / l).astype(o_ref.dtype)


def _fox_flash(q, k, v, c):
    b, s, d = q.shape
    h = c.shape[2]
    dh = d // h
    hp = FOX_HEADS_PER_STEP
    tile = min(FOX_ATTN_TILE, s)
    ct = jnp.swapaxes(c, 1, 2).reshape(b, h // hp, hp, s)
    return pl.pallas_call(
        functools.partial(_fox_flash_kernel, tile=tile, heads=hp),
        grid=(b, h // hp, s // tile),
        in_specs=[
            pl.BlockSpec((None, tile, hp * dh), lambda bi, hi, qi: (bi, qi, hi)),
            pl.BlockSpec((None, s, hp * dh), lambda bi, hi, qi: (bi, 0, hi)),
            pl.BlockSpec((None, s, hp * dh), lambda bi, hi, qi: (bi, 0, hi)),
            pl.BlockSpec((None, None, hp, s), lambda bi, hi, qi: (bi, hi, 0, 0)),
        ],
        out_specs=pl.BlockSpec((None, tile, hp * dh), lambda bi, hi, qi: (bi, qi, hi)),
        out_shape=jax.ShapeDtypeStruct((b, s, d), BF16),
        compiler_params=pltpu.CompilerParams(
            dimension_semantics=("parallel", "parallel", "arbitrary"), vmem_limit_bytes=VMEM_LIMIT_BYTES),
        name="fox_flash",
    )(q, k, v, ct)


def _fox_decode_kernel(pt_ref, q_ref, kn_ref, vn_ref, c0_ref, *refs, pages, heads):
    k_refs = refs[:pages]
    v_refs = refs[pages:2 * pages]
    lf_refs = refs[2 * pages:3 * pages]
    o_ref = refs[3 * pages]
    m_ref, l_ref, acc_ref, carry_ref = refs[3 * pages + 1:]
    p = pl.program_id(1)
    rows = k_refs[0].shape[0]
    width = lf_refs[0].shape[1]

    @pl.when(p == 0)
    def _():
        m_ref[...] = jnp.full_like(m_ref, -jnp.inf)
        l_ref[...] = jnp.zeros_like(l_ref)
        acc_ref[...] = jnp.zeros_like(acc_ref)
        carry_ref[...] = c0_ref[...]

    q = q_ref[...]
    head_of_row = lax.broadcasted_iota(jnp.int32, (heads, rows), 0)
    head_of_col = lax.broadcasted_iota(jnp.int32, (heads, rows), 1) % heads
    own = head_of_row == head_of_col
    lane = lax.broadcasted_iota(jnp.int32, (SUBLANES, width), 1)
    r0 = lax.broadcasted_iota(jnp.int32, (SUBLANES, SUBLANES), 0)
    r1 = lax.broadcasted_iota(jnp.int32, (SUBLANES, SUBLANES), 1)
    later_rows = (r1 > r0).astype(F32)
    all_rows = jnp.ones((SUBLANES, SUBLANES), F32)
    carry = carry_ref[...]
    scores = []
    for g in reversed(range(pages)):
        lf = lf_refs[g][...]
        scan = lf
        total = lf
        d = heads
        while d < width:
            scan = scan + jnp.where(lane + d < width, pltpu.roll(scan, width - d, axis=1), 0.0)
            total = total + pltpu.roll(total, width - d, axis=1)
            d *= 2
        bias = scan - lf + _dot_f32(later_rows, total) + carry
        carry = carry + _dot_f32(all_rows, total)
        bias = jnp.concatenate(
            [jnp.broadcast_to(bias[r:r + 1, :], (heads, width)) for r in range(SUBLANES)], axis=1)
        s = lax.dot_general(q, k_refs[g][...], (((1,), (1,)), ((), ())), preferred_element_type=F32)
        scores.append((g, jnp.where(own, s + bias, -jnp.inf)))
    carry_ref[...] = carry
    m = m_ref[...]
    m_new = m
    for _, s in scores:
        m_new = jnp.maximum(m_new, jnp.max(s, axis=1, keepdims=True))
    alpha = jnp.exp(m - m_new)
    l = alpha * l_ref[...]
    acc = alpha * acc_ref[...]
    for g, s in scores:
        pr = jnp.exp(s - m_new)
        l = l + jnp.sum(pr, axis=1, keepdims=True)
        acc = acc + jnp.dot(pr, v_refs[g][...], preferred_element_type=F32)
    m_ref[...] = m_new
    l_ref[...] = l
    acc_ref[...] = acc

    @pl.when(p == pl.num_programs(1) - 1)
    def _():
        s_new = jnp.sum(_bf(q) * _bf(kn_ref[...]), axis=1, keepdims=True)
        m = m_ref[...]
        m_new = jnp.maximum(m, s_new)
        alpha = jnp.exp(m - m_new)
        p_new = jnp.exp(s_new - m_new)
        l = alpha * l_ref[...] + p_new
        acc = alpha * acc_ref[...] + _bf(p_new) * _bf(vn_ref[...])
        o_ref[...] = acc / l


def _fox_decode(q, k_new, v_new, log_f_new, cache_k, cache_v, cache_logf, page_table):
    b, h, dh = q.shape
    n_pool, ps = cache_k.shape[:2]
    n_pages = page_table.shape[1]
    pages = FOX_PAGES_PER_STEP
    n_steps = n_pages // pages
    rows = ps * h
    width = rows // SUBLANES
    ck = cache_k.reshape(n_pool, rows, dh)
    cv = cache_v.reshape(n_pool, rows, dh)
    clf = cache_logf.reshape(n_pool, SUBLANES, width)
    c0 = jnp.broadcast_to(jnp.tile(log_f_new, (1, width // h))[:, None, :], (b, SUBLANES, width))

    def page_map(g):
        return lambda bi, p, pt: (pt[bi, (n_steps - 1 - p) * pages + g], 0, 0)

    row_spec = pl.BlockSpec((None, h, dh), lambda bi, p, pt: (bi, 0, 0))
    grid_spec = pltpu.PrefetchScalarGridSpec(
        num_scalar_prefetch=1,
        grid=(b, n_steps),
        in_specs=[row_spec, row_spec, row_spec,
                  pl.BlockSpec((None, SUBLANES, width), lambda bi, p, pt: (bi, 0, 0))]
        + [pl.BlockSpec((None, rows, dh), page_map(g)) for g in range(pages)]
        + [pl.BlockSpec((None, rows, dh), page_map(g)) for g in range(pages)]
        + [pl.BlockSpec((None, SUBLANES, width), page_map(g)) for g in range(pages)],
        out_specs=row_spec,
        scratch_shapes=[pltpu.VMEM((h, 1), F32), pltpu.VMEM((h, 1), F32), pltpu.VMEM((h, dh), F32),
                        pltpu.VMEM((SUBLANES, width), F32)],
    )
    return pl.pallas_call(
        functools.partial(_fox_decode_kernel, pages=pages, heads=h),
        grid_spec=grid_spec,
        out_shape=jax.ShapeDtypeStruct((b, h, dh), F32),
        compiler_params=pltpu.CompilerParams(
            dimension_semantics=("parallel", "arbitrary"), vmem_limit_bytes=VMEM_LIMIT_BYTES),
        name="fox_decode",
    )(page_table, q, k_new, v_new, c0, *([ck] * pages), *([cv] * pages), *([clf] * pages))


def _conv_silu_kernel(cur_ref, prev_ref, st_ref, w_ref, o_ref, *, tiles_per_seq, head_dim, post_scale):
    taps = w_ref.shape[0]
    cur = cur_ref[...]
    first = pl.program_id(0) % tiles_per_seq == 0
    prev = jnp.where(first, st_ref[...], prev_ref[...])
    w = w_ref[...]

    def finish(conv):
        y = conv * jax.nn.sigmoid(conv)
        if head_dim is None:
            return y
        parts = []
        for lo in range(0, y.shape[1], head_dim):
            yh = y[:, lo:lo + head_dim]
            yh = yh * lax.rsqrt(jnp.sum(yh * yh, axis=-1, keepdims=True) + NORM_EPS)
            parts.append(yh if post_scale is None else yh * post_scale)
        return jnp.concatenate(parts, axis=1)

    conv = cur * w[taps - 1:taps, :]
    for s in range(1, taps):
        conv = conv + pltpu.roll(cur, s, axis=0) * w[taps - 1 - s:taps - s, :]
    o_ref[...] = finish(conv)
    top = cur[0:SUBLANES, :]
    row = lax.broadcasted_iota(jnp.int32, top.shape, 0)
    conv = top * w[taps - 1:taps, :]
    for s in range(1, taps):
        shifted = jnp.where(row < s, pltpu.roll(prev, s, axis=0), pltpu.roll(top, s, axis=0))
        conv = conv + shifted * w[taps - 1 - s:taps - s, :]
    o_ref[0:SUBLANES, :] = finish(conv)


def _conv_silu(x, state, w, col0, ncols, seq_len, head_dim=None, post_scale=None):
    m, c = x.shape
    b = m // seq_len
    taps = w.shape[0]
    tm = min(ROW_TILE, seq_len)
    tn = min(COL_TILE, ncols)
    tiles_per_seq = seq_len // tm
    j0 = col0 // tn
    st = jnp.pad(state, ((0, 0), (SUBLANES - (taps - 1), 0), (0, 0)))
    return pl.pallas_call(
        functools.partial(_conv_silu_kernel, tiles_per_seq=tiles_per_seq, head_dim=head_dim, post_scale=post_scale),
        grid=(m // tm, ncols // tn),
        in_specs=[
            pl.BlockSpec((tm, tn), lambda i, j: (i, j + j0)),
            pl.BlockSpec((SUBLANES, tn), lambda i, j: (jnp.maximum(i * (tm // SUBLANES) - 1, 0), j + j0)),
            pl.BlockSpec((None, SUBLANES, tn), lambda i, j: (i // tiles_per_seq, 0, j + j0)),
            pl.BlockSpec((taps, tn), lambda i, j: (0, j + j0)),
        ],
        out_specs=pl.BlockSpec((tm, tn), lambda i, j: (i, j)),
        out_shape=jax.ShapeDtypeStruct((m, ncols), F32),
        compiler_params=pltpu.CompilerParams(
            dimension_semantics=("parallel", "parallel"), vmem_limit_bytes=VMEM_LIMIT_BYTES),
        name="conv_silu",
    )(x, x, st, w)


def _gdn_scan_kernel(q_ref, k_ref, v_ref, bcol_ref, gcol_ref, grow_ref, z_ref, gn_ref, s0_ref, o_ref, s_ref, *,
                     rep):
    c = q_ref.shape[0]
    dk = s_ref.shape[1]
    dv = s_ref.shape[2]
    hg = s_ref.shape[0]

    @pl.when(pl.program_id(2) == 0)
    def _():
        s_ref[...] = s0_ref[...]

    row = lax.broadcasted_iota(jnp.int32, (c, c), 0)
    col = lax.broadcasted_iota(jnp.int32, (c, c), 1)
    tril = row >= col
    strict = row > col
    g_cols = _dot_f32(tril.astype(F32), gcol_ref[...])
    g_rows = _dot_f32(grow_ref[...], (row <= col).astype(F32))
    heads = []
    for h in range(hg):
        j = h // rep
        qh = q_ref[:, j * dk:(j + 1) * dk]
        kh = k_ref[:, j * dk:(j + 1) * dk]
        gc = g_cols[:, h:h + 1]
        gr = g_rows[h:h + 1, :]
        decay = jnp.where(tril, jnp.exp(jnp.where(tril, gc - gr, 0.0)), 0.0)
        beta = bcol_ref[:, h:h + 1]
        kb = kh * beta
        exp_g = jnp.exp(gc)
        low = jnp.where(strict, _dot_nt(kb, kh) * decay, 0.0)
        attn = jnp.where(tril, _dot_nt(qh, kh) * decay, 0.0)
        x = jnp.concatenate([v_ref[:, h * dv:(h + 1) * dv] * beta, kb * exp_g], axis=1)
        heads.append(dict(qh=qh, kh=kh, gc=gc, exp_g=exp_g, low=low, attn=attn, x=x))
    for hd in heads:
        hd["x"] = hd["x"] - _dot(hd["low"], hd["x"])
        hd["p"] = hd["low"]
    n = 2
    while n < c:
        for hd in heads:
            hd["p"] = _dot(hd["p"], hd["p"])
        for hd in heads:
            hd["x"] = hd["x"] + _dot(hd["p"], hd["x"])
        n *= 2
    for h, hd in enumerate(heads):
        hd["s"] = s_ref[h]
        hd["v_new"] = hd["x"][:, :dv] - _dot(hd["x"][:, dv:], hd["s"])
    for h, hd in enumerate(heads):
        o = _dot(hd["qh"] * hd["exp_g"], hd["s"]) + _dot(hd["attn"], hd["v_new"])
        o_ref[:, h * dv:(h + 1) * dv] = _gated_rms(o, gn_ref[...], z_ref[:, h * dv:(h + 1) * dv]).astype(o_ref.dtype)
    for h, hd in enumerate(heads):
        gc = hd["gc"]
        g_last = gc[c - 1:c, :]
        s_ref[h] = hd["s"] * jnp.exp(g_last) + _dot_tn(hd["kh"] * jnp.exp(g_last - gc), hd["v_new"])


def _gdn_scan(q, k, v, beta, g, z, o_norm, s0):
    b, t, _ = q.shape
    hv, dk, dv = s0.shape[1:]
    hq = q.shape[2] // dk
    rep = hv // hq
    c = GDN_CHUNK
    nc = t // c
    hg = min(GDN_HEAD_GROUP, hv)
    ng = hv // hg
    cols = lambda a: a.reshape(b, t, ng, hg).transpose(0, 2, 1, 3)
    g_rows = g.reshape(b, nc, c, hv).transpose(0, 1, 3, 2)
    o, s = pl.pallas_call(
        functools.partial(_gdn_scan_kernel, rep=rep),
        grid=(b, ng, nc),
        in_specs=[
            pl.BlockSpec((None, c, hg // rep * dk), lambda bi, gi, ci: (bi, ci, gi)),
            pl.BlockSpec((None, c, hg // rep * dk), lambda bi, gi, ci: (bi, ci, gi)),
            pl.BlockSpec((None, c, hg * dv), lambda bi, gi, ci: (bi, ci, gi)),
            pl.BlockSpec((None, None, c, hg), lambda bi, gi, ci: (bi, gi, ci, 0)),
            pl.BlockSpec((None, None, c, hg), lambda bi, gi, ci: (bi, gi, ci, 0)),
            pl.BlockSpec((None, None, hg, c), lambda bi, gi, ci: (bi, ci, gi, 0)),
            pl.BlockSpec((None, c, hg * dv), lambda bi, gi, ci: (bi, ci, gi)),
            pl.BlockSpec((1, dv), lambda bi, gi, ci: (0, 0)),
            pl.BlockSpec((None, hg, dk, dv), lambda bi, gi, ci: (bi, gi, 0, 0)),
        ],
        out_specs=[
            pl.BlockSpec((None, c, hg * dv), lambda bi, gi, ci: (bi, ci, gi)),
            pl.BlockSpec((None, hg, dk, dv), lambda bi, gi, ci: (bi, gi, 0, 0)),
        ],
        out_shape=[jax.ShapeDtypeStruct((b, t, hv * dv), BF16), jax.ShapeDtypeStruct((b, hv, dk, dv), F32)],
        compiler_params=pltpu.CompilerParams(
            dimension_semantics=("parallel", "parallel", "arbitrary"), vmem_limit_bytes=VMEM_LIMIT_BYTES),
        name="gdn_scan",
    )(q, k, v, cols(beta), cols(g), g_rows, z, o_norm.reshape(1, dv), s0)
    return o, s


def _ret_scan_kernel(lg_ref, q_ref, k_ref, v_ref, gate_ref, gn_ref, s0_ref, o_ref, s_ref):
    c = q_ref.shape[0]
    heads, dk, dv = s_ref.shape

    @pl.when(pl.program_id(2) == 0)
    def _():
        s_ref[...] = s0_ref[...]

    row = lax.broadcasted_iota(jnp.int32, (c, c), 0)
    col = lax.broadcasted_iota(jnp.int32, (c, c), 1)
    tril = row >= col
    dist = jnp.where(tril, row - col, 0).astype(F32)
    pos = lax.broadcasted_iota(jnp.int32, (c, 1), 0).astype(F32)
    hs = []
    for h in range(heads):
        lg = lg_ref[h]
        q = q_ref[:, h * dk:(h + 1) * dk]
        k = k_ref[:, h * dk:(h + 1) * dk]
        qk = lax.dot_general(q, k, (((1,), (1,)), ((), ())), preferred_element_type=F32)
        hs.append(dict(lg=lg, q=q, k=k, qk=qk * jnp.where(tril, jnp.exp(lg * dist), 0.0), s=s_ref[h]))
    for h, hd in enumerate(hs):
        v = v_ref[:, h * dv:(h + 1) * dv]
        inner = jnp.dot(hd["qk"], v, preferred_element_type=F32)
        cross = jnp.dot(hd["q"] * jnp.exp(hd["lg"] * (pos + 1.0)), hd["s"], preferred_element_type=F32)
        gate = gate_ref[:, h * dv:(h + 1) * dv]
        o_ref[:, h * dv:(h + 1) * dv] = _gated_group_norm(inner + cross, gn_ref[h], gate).astype(o_ref.dtype)
    for h, hd in enumerate(hs):
        lg = hd["lg"]
        kd = hd["k"] * jnp.exp(lg * (c - 1.0 - pos))
        s_ref[h] = hd["s"] * jnp.exp(lg * c) + lax.dot_general(
            kd, v_ref[:, h * dv:(h + 1) * dv], (((0,), (0,)), ((), ())), preferred_element_type=F32)


def _ret_scan(q, k, v, gate, gn_gain, s0, log_gamma):
    b, t, _ = q.shape
    h, dk, dv = s0.shape[1:]
    c = RET_CHUNK
    hp = RET_HEADS_PER_STEP
    seq = lambda width: pl.BlockSpec((None, c, hp * width), lambda bi, hi, ci: (bi, ci, hi))
    state = pl.BlockSpec((None, hp, dk, dv), lambda bi, hi, ci: (bi, hi, 0, 0))
    return pl.pallas_call(
        _ret_scan_kernel,
        grid=(b, h // hp, t // c),
        in_specs=[
            pl.BlockSpec((hp, 1, 1), lambda bi, hi, ci: (hi, 0, 0)),
            seq(dk), seq(dk), seq(dv), seq(dv),
            pl.BlockSpec((hp, 1, dv), lambda bi, hi, ci: (hi, 0, 0)),
            state,
        ],
        out_specs=[seq(dv), state],
        out_shape=[jax.ShapeDtypeStruct((b, t, h * dv), BF16), jax.ShapeDtypeStruct((b, h, dk, dv), F32)],
        compiler_params=pltpu.CompilerParams(
            dimension_semantics=("parallel", "parallel", "arbitrary"), vmem_limit_bytes=VMEM_LIMIT_BYTES),
        name="ret_scan",
    )(log_gamma.reshape(h, 1, 1), q, k, v, gate, gn_gain.reshape(h, 1, dv), s0)


def _first_row(a):
    row = lax.broadcasted_iota(jnp.int32, (SUBLANES, a.shape[1]), 0)
    return jnp.where(row == 0, jnp.broadcast_to(a, (SUBLANES, a.shape[1])), 0.0)


def _gdn_step_kernel(q_ref, k_ref, v_ref, beta_ref, g_ref, z_ref, gn_ref, s0_ref, o_ref, s_ref, *, rep):
    hv = s0_ref.shape[0]
    for h in range(hv):
        j = h // rep
        q = q_ref[j:j + 1, :]
        k = k_ref[j:j + 1, :]
        v = v_ref[h:h + 1, :]
        beta = beta_ref[h:h + 1, :]
        eg = jnp.exp(g_ref[h:h + 1, :])
        s = s0_ref[h]
        kb = k * beta
        lhs = jnp.concatenate([kb * eg, q * eg, jnp.zeros((SUBLANES - 2, q.shape[1]), F32)], axis=0)
        both = jnp.dot(lhs, s, preferred_element_type=F32)
        v_new = v * beta - both[0:1, :]
        attn = jnp.sum(_bf(q) * _bf(k), axis=1, keepdims=True)
        o = both[1:2, :] + _bf(attn) * _bf(v_new)
        o_ref[h:h + 1, :] = _gated_rms(o, gn_ref[...], z_ref[h:h + 1, :])
        s_ref[h] = s * eg + lax.dot_general(_first_row(k), _first_row(v_new), (((0,), (0,)), ((), ())),
                                            preferred_element_type=F32)


def _gdn_step(q, k, v, beta, g, z, o_norm, s0):
    b, hq, dk = q.shape
    hv, dv = v.shape[1:]
    spec = lambda *shape: pl.BlockSpec((None,) + shape, lambda bi: (bi,) + (0,) * len(shape))
    return pl.pallas_call(
        functools.partial(_gdn_step_kernel, rep=hv // hq),
        grid=(b,),
        in_specs=[spec(hq, dk), spec(hq, dk), spec(hv, dv), spec(hv, 1), spec(hv, 1), spec(hv, dv),
                  pl.BlockSpec((1, dv), lambda bi: (0, 0)), spec(hv, dk, dv)],
        out_specs=[spec(hv, dv), spec(hv, dk, dv)],
        out_shape=[jax.ShapeDtypeStruct((b, hv, dv), F32), jax.ShapeDtypeStruct((b, hv, dk, dv), F32)],
        compiler_params=pltpu.CompilerParams(dimension_semantics=("parallel",), vmem_limit_bytes=VMEM_LIMIT_BYTES),
        name="gdn_step",
    )(q, k, v, beta[..., None], g[..., None], z, o_norm.reshape(1, dv), s0)


def _ret_step_kernel(lg_ref, q_ref, k_ref, v_ref, gate_ref, gn_ref, s0_ref, o_ref, s_ref):
    gamma = jnp.exp(lg_ref[...])
    q = q_ref[...]
    k = k_ref[...]
    v = v_ref[...]
    s = s0_ref[...]
    cross = jnp.dot(_first_row(q * gamma), s, preferred_element_type=F32)[0:1, :]
    attn = jnp.sum(_bf(q) * _bf(k), axis=1, keepdims=True)
    o_ref[...] = _gated_group_norm(_bf(attn) * _bf(v) + cross, gn_ref[...], gate_ref[...])
    s_ref[...] = s * gamma + lax.dot_general(_first_row(k), _first_row(v), (((0,), (0,)), ((), ())),
                                             preferred_element_type=F32)


def _ret_step(q, k, v, gate, gn_gain, s0, log_gamma):
    b, h, dk = q.shape
    dv = v.shape[2]
    vec = lambda n: pl.BlockSpec((None, None, 1, n), lambda bi, hi: (bi, hi, 0, 0))
    mat = pl.BlockSpec((None, None, dk, dv), lambda bi, hi: (bi, hi, 0, 0))
    o, s = pl.pallas_call(
        _ret_step_kernel,
        grid=(b, h),
        in_specs=[pl.BlockSpec((None, 1, 1), lambda bi, hi: (hi, 0, 0)), vec(dk), vec(dk), vec(dv), vec(dv),
                  pl.BlockSpec((None, 1, dv), lambda bi, hi: (hi, 0, 0)), mat],
        out_specs=[vec(dv), mat],
        out_shape=[jax.ShapeDtypeStruct((b, h, 1, dv), F32), jax.ShapeDtypeStruct((b, h, dk, dv), F32)],
        compiler_params=pltpu.CompilerParams(
            dimension_semantics=("parallel", "parallel"), vmem_limit_bytes=VMEM_LIMIT_BYTES),
        name="ret_step",
    )(log_gamma.reshape(h, 1, 1), q[:, :, None, :], k[:, :, None, :], v[:, :, None, :], gate[:, :, None, :],
      gn_gain.reshape(h, 1, dv), s0)
    return o[:, :, 0, :], s


def _l2_norm(x):
    return x * lax.rsqrt(jnp.sum(x * x, axis=-1, keepdims=True) + NORM_EPS)


def _fox_mixer(x, b, t, g, w_in, b_f, q_norm, k_norm, w_o, cache):
    d = D_MODEL
    prompt = cache is None
    scale = FOX_HEAD_DIM ** -0.5
    head_rms = dict(epilogue="head_rms", head_dim=FOX_HEAD_DIM)
    kv_dtypes = (F32, BF16) if prompt else (F32,)
    *k, xn = _proj(x, w_in[:, d:2 * d], gain=g, extra=(k_norm,), out_dtypes=kv_dtypes, emit_xn=True, **head_rms)
    v = _proj(xn, w_in[:, 2 * d:3 * d], out_dtypes=kv_dtypes)
    log_f = _proj(xn, _pad_cols(w_in[:, 3 * d:]), epilogue="log_sigmoid", extra=(_pad_cols(b_f),))
    log_f = log_f[:, :FOX_HEADS].reshape(b, t, FOX_HEADS)
    if prompt:
        (k, k_bf), (v, v_bf) = k, v
        q_bf = _proj(xn, w_in[:, :d], extra=(q_norm,), out_dtypes=(BF16,), post_scale=scale * LOG2E, **head_rms)
        seq = lambda a: a.reshape(b, t, d)
        o = _fox_flash(seq(q_bf), seq(k_bf), seq(v_bf), jnp.cumsum(log_f, axis=1))
    else:
        assert t == 1
        (k,) = k
        q = _proj(xn, w_in[:, :d], extra=(q_norm,), post_scale=scale, **head_rms)
        heads = lambda a: a.reshape(b, FOX_HEADS, FOX_HEAD_DIM)
        o = _fox_decode(heads(q), heads(k), heads(v), log_f[:, 0], *cache).astype(BF16)
    y = _proj(o.reshape(b * t, d), w_o, residual=x)
    hd = (b, t, FOX_HEADS, FOX_HEAD_DIM)
    return y, (k.reshape(hd), v.reshape(hd), log_f)


def _gdn_mixer(x, b, t, g, w_in, conv_w, a_log, dt_bias, o_norm, w_o, conv_buf, s0):
    n_main = GDN_CONV_DIM + GDN_V_DIM
    qkv, xn = _proj(x, w_in[:, :GDN_CONV_DIM], gain=g, emit_xn=True)
    z = _proj(xn, w_in[:, GDN_CONV_DIM:n_main])
    tail = _proj(xn, _pad_cols(w_in[:, n_main:]))
    b_in = tail[:, :GDN_V_HEADS].reshape(b, t, GDN_V_HEADS)
    a_in = tail[:, GDN_V_HEADS:2 * GDN_V_HEADS].reshape(b, t, GDN_V_HEADS)
    beta = jax.nn.sigmoid(b_in)
    gg = -jnp.exp(a_log) * jax.nn.softplus(a_in + dt_bias)
    if t % GDN_CHUNK == 0:
        conv = functools.partial(_conv_silu, qkv, conv_buf, conv_w, seq_len=t)
        q = conv(0, GDN_QK_DIM, head_dim=GDN_DK, post_scale=GDN_DK ** -0.5)
        k = conv(GDN_QK_DIM, GDN_QK_DIM, head_dim=GDN_DK)
        v = conv(2 * GDN_QK_DIM, GDN_V_DIM)
        new_buf = qkv.reshape(b, t, GDN_CONV_DIM)[:, t - (GDN_CONV_WIDTH - 1):]
        seq = lambda a: a.reshape(b, t, a.shape[1])
        o, s_new = _gdn_scan(seq(q), seq(k), seq(v), beta, gg, seq(z), o_norm, s0)
    else:
        assert t == 1
        xc = jnp.concatenate([conv_buf, qkv[:, None, :]], axis=1)
        conv = xc[:, 0] * conv_w[0]
        for w in range(1, GDN_CONV_WIDTH):
            conv = conv + xc[:, w] * conv_w[w]
        conv = jax.nn.silu(conv)
        new_buf = xc[:, 1:]
        heads = lambda a: a.reshape(b, a.shape[1] // GDN_DK, GDN_DK)
        q = _l2_norm(heads(conv[:, :GDN_QK_DIM])) * GDN_DK ** -0.5
        k = _l2_norm(heads(conv[:, GDN_QK_DIM:2 * GDN_QK_DIM]))
        o, s_new = _gdn_step(q, k, heads(conv[:, 2 * GDN_QK_DIM:]), beta[:, 0], gg[:, 0], heads(z), o_norm, s0)
    y = _proj(o.reshape(b * t, GDN_V_DIM).astype(BF16), w_o, residual=x)
    return y, (new_buf, s_new)


def _ret_mixer(x, b, t, g, w_in, gn_gain, w_o, s0, pos0):
    d = D_MODEL
    half = RET_DK // 2
    inv = ROPE_BASE ** (-jnp.arange(half, dtype=F32) / half)
    ang = (pos0 + jnp.arange(t)).astype(F32)[:, None] * inv[None, :]
    rotary = dict(epilogue="rotary", extra=(jnp.tile(jnp.cos(ang), (b, 1)), jnp.tile(jnp.sin(ang), (b, 1))),
                  head_dim=RET_DK)
    q, xn = _proj(x, w_in[:, :d], gain=g, emit_xn=True, **rotary)
    k = _proj(xn, w_in[:, d:2 * d], post_scale=RET_DK ** -0.5, **rotary)
    v = _proj(xn, w_in[:, 2 * d:2 * d + RET_V_DIM])
    gate = _proj(xn, w_in[:, 2 * d + RET_V_DIM:])
    log_gamma = jnp.log1p(-jnp.exp2(-5.0 - jnp.arange(RET_HEADS, dtype=F32)))
    if t % RET_CHUNK == 0:
        seq = lambda a: a.reshape(b, t, a.shape[1])
        o, s_new = _ret_scan(seq(q), seq(k), seq(v), seq(gate), gn_gain, s0, log_gamma)
    else:
        assert t == 1
        heads = lambda a: a.reshape(b, RET_HEADS, a.shape[1] // RET_HEADS)
        o, s_new = _ret_step(heads(q), heads(k), heads(v), heads(gate), gn_gain, s0, log_gamma)
    y = _proj(o.reshape(b * t, RET_V_DIM).astype(BF16), w_o, residual=x)
    return y, (s_new,)


def kernel(x_prompt, x_sample, cache_k_l0, cache_v_l0, cache_logf_l0, state_conv_l1, state_gdn_l1, state_ret_l2, cache_k_l3, cache_v_l3, cache_logf_l3, page_table, norm_g, ffn_w_in, ffn_w_out, fox_w_in_l0, fox_b_f_l0, fox_q_norm_l0, fox_k_norm_l0, fox_w_o_l0, gdn_w_in_l1, gdn_conv_w_l1, gdn_a_log_l1, gdn_dt_bias_l1, gdn_o_norm_l1, gdn_w_o_l1, ret_w_in_l2, ret_gn_l2, ret_w_o_l2, fox_w_in_l3, fox_b_f_l3, fox_q_norm_l3, fox_k_norm_l3, fox_w_o_l3):
    ffn_bf16 = {}
    fox0 = (fox_w_in_l0.astype(BF16), fox_b_f_l0, fox_q_norm_l0, fox_k_norm_l0, fox_w_o_l0.astype(BF16))
    fox3 = (fox_w_in_l3.astype(BF16), fox_b_f_l3, fox_q_norm_l3, fox_k_norm_l3, fox_w_o_l3.astype(BF16))
    gdn = (gdn_w_in_l1.astype(BF16), gdn_conv_w_l1, gdn_a_log_l1, gdn_dt_bias_l1, gdn_o_norm_l1,
           gdn_w_o_l1.astype(BF16))
    ret = (ret_w_in_l2.astype(BF16), ret_gn_l2, ret_w_o_l2.astype(BF16))

    def ffn(x, sample, i, which):
        g = norm_g[i, 2 * which]
        if sample:
            x, ffn_bf16[i, which] = _ffn_casting(x, g, ffn_w_in, ffn_w_out, i, which)
            return x
        return _ffn(x, g, ffn_bf16[i, which])

    def run(x3, sample):
        b, t, _ = x3.shape
        x = x3.reshape(b * t, D_MODEL)
        states = []
        for i in range(DEPTH):
            x = ffn(x, sample, i, 0)
            g = norm_g[i, 1]
            if i == 0:
                cache = (cache_k_l0, cache_v_l0, cache_logf_l0, page_table) if sample else None
                x, st = _fox_mixer(x, b, t, g, *fox0, cache)
            elif i == 1:
                if sample:
                    conv0, s0 = state_conv_l1, state_gdn_l1
                else:
                    conv0 = jnp.zeros((b, GDN_CONV_WIDTH - 1, GDN_CONV_DIM), F32)
                    s0 = jnp.zeros((b, GDN_V_HEADS, GDN_DK, GDN_DV), F32)
                x, st = _gdn_mixer(x, b, t, g, *gdn, conv0, s0)
            elif i == 2:
                s0 = state_ret_l2 if sample else jnp.zeros((b, RET_HEADS, RET_DK, RET_DV), F32)
                x, st = _ret_mixer(x, b, t, g, *ret, s0, PAST_LEN if sample else 0)
            else:
                cache = (cache_k_l3, cache_v_l3, cache_logf_l3, page_table) if sample else None
                x, st = _fox_mixer(x, b, t, g, *fox3, cache)
            x = ffn(x, sample, i, 1)
            states.append(st)
        return x.reshape(b, t, D_MODEL), states

    ys, st_s = run(x_sample, True)
    yp, st_p = run(x_prompt, False)
    new_state = []
    for sp, ss in zip(st_p, st_s):
        new_state += [*sp, *ss]
    return (yp, ys, *new_state)
```
